```python
import jax, jax.numpy as jnp
from jax import lax
import numpy as np

D_MODEL = 1024
BATCH = 8
SEQ = 8192
DEPTH = 1

GRID_W = 64
CTX_LEN = 256
N_HEADS = 8
QK_NOPE = 64
QK_ROPE = 32
ROPE_AXIS = QK_ROPE // 2
V_DIM = 64
Q_LORA = 384
KV_LORA = 256
MLA_WIDTH = N_HEADS * V_DIM
POOL_WINDOWS = (2, 4, 8, 16)
POOL_GROUPS = len(POOL_WINDOWS)
POOL_WIDTH = 512
POOL_GC = POOL_WIDTH // POOL_GROUPS
D_FF = 4 * D_MODEL
IN_SPLITS = (Q_LORA,
             Q_LORA + KV_LORA,
             Q_LORA + KV_LORA + QK_ROPE,
             Q_LORA + KV_LORA + QK_ROPE + POOL_WIDTH,
             Q_LORA + KV_LORA + QK_ROPE + POOL_WIDTH + D_MODEL)
IN_WIDTH = Q_LORA + KV_LORA + QK_ROPE + POOL_WIDTH + 2 * D_MODEL
Q_BLOCK = 128
ROPE_THETA = 10000.0
NORM_EPS = 1e-6
ATTN_SCALE = (QK_NOPE + QK_ROPE) ** -0.5

kernel_name = 'hybrid_mla_pool_dit_block'


def rmsnorm(x, g):
    xf = x.astype(jnp.float32)
    y = xf * lax.rsqrt(jnp.mean(xf * xf, axis=-1, keepdims=True) + NORM_EPS)
    return (y * g.astype(jnp.float32)).astype(x.dtype)


def modulate(x, g, shift, scale):
    return rmsnorm(x, g) * (1 + scale) + shift


def adaln(cond, w_ada, b_ada):
    return jnp.split(jax.nn.silu(cond) @ w_ada + b_ada, 6, axis=-1)


def rotate(x, ang):
    half = x.shape[-1] // 2
    cos = jnp.cos(ang).astype(x.dtype)
    sin = jnp.sin(ang).astype(x.dtype)
    x1, x2 = x[..., :half], x[..., half:]
    return jnp.concatenate([x1 * cos - x2 * sin, x1 * sin + x2 * cos], axis=-1)


def axial_rope(x, ang_row, ang_col):
    return jnp.concatenate([rotate(x[..., :ROPE_AXIS], ang_row),
                            rotate(x[..., ROPE_AXIS:], ang_col)], axis=-1)


def mixer_inputs(h, w_in, q_norm_g, kv_norm_g, w_uq, w_ukv):
    b, l, _ = h.shape
    c_q, c_kv, k_rope, pool_in, g_mla, g_pool = jnp.split(h @ w_in, IN_SPLITS, axis=-1)
    q = (rmsnorm(c_q, q_norm_g) @ w_uq).reshape(b, l, N_HEADS, QK_NOPE + QK_ROPE)
    kv = (rmsnorm(c_kv, kv_norm_g) @ w_ukv).reshape(b, l, N_HEADS, QK_NOPE + V_DIM)
    q_nope, q_rope = q[..., :QK_NOPE], q[..., QK_NOPE:]
    k_nope, v = kv[..., :QK_NOPE], kv[..., QK_NOPE:]
    return q_nope, q_rope, k_nope, k_rope, v, pool_in, g_mla, g_pool


def attend(q_nope, q_rope, k_nope, k_rope, v):
    b, lq, h, _ = q_nope.shape
    nb = lq // Q_BLOCK
    qn = q_nope.reshape(b, nb, Q_BLOCK, h, QK_NOPE).swapaxes(0, 1)
    qr = q_rope.reshape(b, nb, Q_BLOCK, h, QK_ROPE).swapaxes(0, 1)

    def block(args):
        qn_b, qr_b = args
        s = (jnp.einsum('bqhd,bkhd->bhqk', qn_b, k_nope)
             + jnp.einsum('bqhr,bkr->bhqk', qr_b, k_rope))
        p = jax.nn.softmax(s.astype(jnp.float32) * ATTN_SCALE, axis=-1).astype(v.dtype)
        return jnp.einsum('bhqk,bkhd->bqhd', p, v)

    o = lax.map(block, (qn, qr))
    return o.swapaxes(0, 1).reshape(b, lq, h * V_DIM)


def multiscale_pool(u, pool_w, pool_scale):
    b, l, _ = u.shape
    uf = u.astype(jnp.float32)
    cs = jnp.concatenate([jnp.zeros((b, 1, POOL_WIDTH), jnp.float32),
                          jnp.cumsum(uf, axis=1)], axis=1)
    t = jnp.arange(l)
    outs = []
    for g, w in enumerate(POOL_WINDOWS):
        lo = jnp.clip(t - w // 2, 0, l)
        hi = jnp.clip(t + w // 2, 0, l)
        csg = cs[..., g * POOL_GC:(g + 1) * POOL_GC]
        cnt = (hi - lo).astype(jnp.float32)[None, :, None]
        outs.append((csg[:, hi] - csg[:, lo]) / cnt - uf[..., g * POOL_GC:(g + 1) * POOL_GC])
    d = jnp.stack(outs, axis=2).astype(u.dtype)
    y = jnp.einsum('blgc,gcd->blgd', d, pool_w).reshape(b, l, POOL_WIDTH)
    return y * pool_scale


def merge_branches(attn, pool_in, g_mla, g_pool, pool_w, pool_scale, w_br_mla, w_br_pool, w_out):
    pooled = multiscale_pool(pool_in, pool_w, pool_scale)
    merged = (jax.nn.sigmoid(g_mla) * (attn @ w_br_mla)
              + jax.nn.sigmoid(g_pool) * (pooled @ w_br_pool))
    return merged @ w_out


def channel_mlp(h, w1, w2):
    return jnp.square(jax.nn.relu(h @ w1)) @ w2


def setup_inputs(seed: int = 0) -> dict:
    key = jax.random.key(seed)
    ks = jax.random.split(key, 24)

    def nrm(k, shape, fan_in, s=1.0):
        return jax.random.normal(k, shape, jnp.float32) * (s * fan_in ** -0.5)

    def gain(k, shape):
        return 1.0 + 0.05 * jax.random.normal(k, shape, jnp.float32)

    L = DEPTH
    return {
        'x': jax.random.normal(ks[0], (BATCH, SEQ, D_MODEL), jnp.float32),
        'c': jax.random.normal(ks[1], (BATCH, D_MODEL), jnp.float32),
        'ctx': jax.random.normal(ks[2], (BATCH, CTX_LEN, D_MODEL), jnp.float32),
        'c_ctx': jax.random.normal(ks[3], (D_MODEL,), jnp.float32),
        'w_ada': nrm(ks[4], (L, D_MODEL, 6 * D_MODEL), D_MODEL, 0.5),
        'b_ada': 0.02 * jax.random.normal(ks[5], (L, 6 * D_MODEL), jnp.float32),
        'norm1_g': gain(ks[6], (L, D_MODEL)),
        'w_in': nrm(ks[7], (L, D_MODEL, IN_WIDTH), D_MODEL),
        'q_norm_g': gain(ks[8], (L, Q_LORA)),
        'kv_norm_g': gain(ks[9], (L, KV_LORA)),
        'w_uq': nrm(ks[10], (L, Q_LORA, N_HEADS * (QK_NOPE + QK_ROPE)), Q_LORA),
        'w_ukv': nrm(ks[11], (L, KV_LORA, N_HEADS * (QK_NOPE + V_DIM)), KV_LORA),
        'w_br_mla': nrm(ks[12], (L, MLA_WIDTH, D_MODEL), MLA_WIDTH),
        'pool_w': nrm(ks[13], (L, POOL_GROUPS, POOL_GC, POOL_GC), POOL_GC),
        'pool_scale': gain(ks[14], (L, POOL_WIDTH)),
        'w_br_pool': nrm(ks[15], (L, POOL_WIDTH, D_MODEL), POOL_WIDTH),
        'w_out': nrm(ks[16], (L, D_MODEL, D_MODEL), D_MODEL),
        'norm2_g': gain(ks[17], (L, D_MODEL)),
        'w_mlp1': nrm(ks[18], (L, D_MODEL, D_FF), D_MODEL),
        'w_mlp2': nrm(ks[19], (L, D_FF, D_MODEL), D_FF),
        'final_g': gain(ks[20], (D_MODEL,)),
    }


def reference(x, c, ctx, c_ctx, w_ada, b_ada, norm1_g, w_in, q_norm_g, kv_norm_g, w_uq, w_ukv,
              w_br_mla, pool_w, pool_scale, w_br_pool, w_out, norm2_g, w_mlp1, w_mlp2, final_g):
    seq_len = x.shape[1]
    n_rows = seq_len // GRID_W
    row = jnp.repeat(jnp.arange(n_rows, dtype=jnp.float32), GRID_W)
    col = jnp.tile(jnp.arange(GRID_W, dtype=jnp.float32), n_rows)
    inv_freq = ROPE_THETA ** (-jnp.arange(0, ROPE_AXIS, 2, dtype=jnp.float32) / ROPE_AXIS)
    ang_row = row[:, None] * inv_freq
    ang_col = col[:, None] * inv_freq

    for i in range(DEPTH):
        sh1, sc1, gt1, sh2, sc2, gt2 = adaln(c, w_ada[i], b_ada[i])
        csh1, csc1, cgt1, csh2, csc2, cgt2 = adaln(c_ctx, w_ada[i], b_ada[i])

        h_ctx = modulate(ctx, norm1_g[i], csh1, csc1)
        qn_c, qr_c, kn_c, kr_c, v_c, pool_c, gm_c, gp_c = mixer_inputs(
            h_ctx, w_in[i], q_norm_g[i], kv_norm_g[i], w_uq[i], w_ukv[i])

        h = modulate(x, norm1_g[i], sh1[:, None], sc1[:, None])
        qn, qr, kn, kr, v, pool_x, gm, gp = mixer_inputs(
            h, w_in[i], q_norm_g[i], kv_norm_g[i], w_uq[i], w_ukv[i])
        qr = axial_rope(qr, ang_row[None, :, None], ang_col[None, :, None])
        kr = axial_rope(kr, ang_row[None], ang_col[None])

        attn = attend(qn, qr,
                      jnp.concatenate([kn_c, kn], axis=1),
                      jnp.concatenate([kr_c, kr], axis=1),
                      jnp.concatenate([v_c, v], axis=1))
        x = x + gt1[:, None] * merge_branches(attn, pool_x, gm, gp, pool_w[i], pool_scale[i],
                                              w_br_mla[i], w_br_pool[i], w_out[i])
        h2 = modulate(x, norm2_g[i], sh2[:, None], sc2[:, None])
        x = x + gt2[:, None] * channel_mlp(h2, w_mlp1[i], w_mlp2[i])

        if i < DEPTH - 1:
            attn_c = attend(qn_c, qr_c, kn_c, kr_c, v_c)
            ctx = ctx + cgt1 * merge_branches(attn_c, pool_c, gm_c, gp_c, pool_w[i], pool_scale[i],
                                              w_br_mla[i], w_br_pool[i], w_out[i])
            h2c = modulate(ctx, norm2_g[i], csh2, csc2)
            ctx = ctx + cgt2 * channel_mlp(h2c, w_mlp1[i], w_mlp2[i])

    return rmsnorm(x, final_g)
```

```python
import functools
import math

import jax
import jax.numpy as jnp
import numpy as np
from jax import lax
from jax.experimental import pallas as pl
from jax.experimental.pallas import tpu as pltpu

D_MODEL = 1024
GRID_W = 64
N_HEADS = 8
QK_NOPE = 64
QK_ROPE = 32
ROPE_AXIS = QK_ROPE // 2
V_DIM = 64
Q_LORA = 384
KV_LORA = 256
POOL_WINDOWS = (2, 4, 8, 16)
POOL_WIDTH = 512
POOL_GC = POOL_WIDTH // len(POOL_WINDOWS)
D_FF = 4 * D_MODEL
ROPE_THETA = 10000.0
NORM_EPS = 1e-6
ATTN_SCALE = (QK_NOPE + QK_ROPE) ** -0.5

HEAD_SLAB = 128
POOL_HALO = max(POOL_WINDOWS) // 2
VMEM_LIMIT_BYTES = 56 * 1024 * 1024

BF16 = jnp.bfloat16
F32 = jnp.float32


def _dot(a, b):
    return jnp.dot(a, b, preferred_element_type=F32)


def _dot_nt(a, b):
    return lax.dot_general(a, b, (((1,), (1,)), ((), ())), preferred_element_type=F32)


def _rms(x, g):
    ms = jnp.mean(x * x, axis=-1, keepdims=True)
    return x * lax.rsqrt(ms + NORM_EPS) * g


def _modulated(x, g, shift, scale):
    return _rms(x, g) * (1.0 + scale) + shift


def _adaln_kernel(cond_ref, w_ref, b_ref, o_ref):
    c = cond_ref[...]
    act = (c * jax.nn.sigmoid(c)).astype(BF16)
    o_ref[...] = _dot(act, w_ref[...].astype(BF16)) + b_ref[...]


def _adaln(cond, w_ada, b_ada):
    rows, d = cond.shape
    n = w_ada.shape[1]
    tn = 1536
    return pl.pallas_call(
        _adaln_kernel,
        out_shape=jax.ShapeDtypeStruct((rows, n), F32),
        grid=(n // tn,),
        in_specs=[pl.BlockSpec((rows, d), lambda j: (0, 0)),
                  pl.BlockSpec((d, tn), lambda j: (0, j)),
                  pl.BlockSpec((1, tn), lambda j: (0, j))],
        out_specs=pl.BlockSpec((rows, tn), lambda j: (0, j)),
        compiler_params=pltpu.CompilerParams(
            dimension_semantics=("arbitrary",), vmem_limit_bytes=VMEM_LIMIT_BYTES),
        name="adaln",
    )(cond, w_ada, b_ada)


_A_CQ = (0, Q_LORA)
_A_CKV = (Q_LORA, Q_LORA + KV_LORA)
_A_KR = (_A_CKV[1], _A_CKV[1] + HEAD_SLAB)
_A_KRS = (_A_KR[1], _A_KR[1] + HEAD_SLAB)
_A_POOL = (_A_KRS[1], _A_KRS[1] + POOL_WIDTH)
_A_WIDTH = _A_POOL[1]


def _inproj_latent_kernel(x_ref, mod_ref, g1_ref, wa_ref, qg_ref, kvg_ref, wq_ref, wk_ref, wvt_ref,
                          cos_ref, sin_ref, q_ref, k_ref, vt_ref, pool_ref):
    x = x_ref[0]
    mod = mod_ref[0]
    h = _modulated(x, g1_ref[...], mod[0:1], mod[1:2]).astype(BF16)
    a = _dot(h, wa_ref[...])
    pool_ref[0] = a[:, _A_POOL[0]:_A_POOL[1]]
    cos = cos_ref[...]
    sin = sin_ref[...]

    cqn = _rms(a[:, _A_CQ[0]:_A_CQ[1]], qg_ref[...]).astype(BF16)
    qq = _dot(cqn, wq_ref[...])
    hw = N_HEADS * HEAD_SLAB
    for hd in range(N_HEADS):
        lo = hd * HEAD_SLAB
        qh = qq[:, lo:lo + HEAD_SLAB] * cos + qq[:, hw + lo:hw + lo + HEAD_SLAB] * sin
        q_ref[0, hd] = qh.astype(BF16)

    ckvn = _rms(a[:, _A_CKV[0]:_A_CKV[1]], kvg_ref[...]).astype(BF16)
    kk = _dot(ckvn, wk_ref[...])
    kr = a[:, _A_KR[0]:_A_KR[1]] * cos + a[:, _A_KRS[0]:_A_KRS[1]] * sin
    for hd in range(N_HEADS):
        lo = hd * HEAD_SLAB
        k_ref[0, hd] = (kk[:, lo:lo + HEAD_SLAB] + kr).astype(BF16)
    vt_ref[0, 0] = _dot_nt(wvt_ref[...], ckvn).astype(BF16)


def _inproj_ctx_kernel(x_ref, mod_ref, g1_ref, wa_ref, kvg_ref, wk_ref, wvt_ref, k_ref, vt_ref):
    x = x_ref[0]
    mod = mod_ref[0]
    h = _modulated(x, g1_ref[...], mod[0:1], mod[1:2]).astype(BF16)
    a = _dot(h, wa_ref[:, _A_CKV[0]:_A_KR[1]])
    ckvn = _rms(a[:, :KV_LORA], kvg_ref[...]).astype(BF16)
    kk = _dot(ckvn, wk_ref[...])
    kr = a[:, KV_LORA:KV_LORA + HEAD_SLAB]
    for hd in range(N_HEADS):
        lo = hd * HEAD_SLAB
        k_ref[0, hd] = (kk[:, lo:lo + HEAD_SLAB] + kr).astype(BF16)
    vt_ref[0, 0] = _dot_nt(wvt_ref[...], ckvn).astype(BF16)


def _const_spec(shape):
    nd = len(shape)
    return pl.BlockSpec(shape, lambda *_: (0,) * nd)


def _inproj_latent(x, mod, g1, wa, qg, kvg, wq, wk, wvt, cos, sin, tm):
    b, l, d = x.shape
    nt = l // tm
    return pl.pallas_call(
        _inproj_latent_kernel,
        out_shape=(jax.ShapeDtypeStruct((b, N_HEADS, l, HEAD_SLAB), BF16),
                   jax.ShapeDtypeStruct((b, N_HEADS, l, HEAD_SLAB), BF16),
                   jax.ShapeDtypeStruct((b, nt, N_HEADS * V_DIM, tm), BF16),
                   jax.ShapeDtypeStruct((b, l, POOL_WIDTH), F32)),
        grid=(b, nt),
        in_specs=[pl.BlockSpec((1, tm, d), lambda bi, i: (bi, i, 0)),
                  pl.BlockSpec((1, 6, d), lambda bi, i: (bi, 0, 0)),
                  _const_spec(g1.shape), _const_spec(wa.shape), _const_spec(qg.shape),
                  _const_spec(kvg.shape), _const_spec(wq.shape), _const_spec(wk.shape),
                  _const_spec(wvt.shape),
                  pl.BlockSpec((tm, HEAD_SLAB), lambda bi, i: (i, 0)),
                  pl.BlockSpec((tm, HEAD_SLAB), lambda bi, i: (i, 0))],
        out_specs=(pl.BlockSpec((1, N_HEADS, tm, HEAD_SLAB), lambda bi, i: (bi, 0, i, 0)),
                   pl.BlockSpec((1, N_HEADS, tm, HEAD_SLAB), lambda bi, i: (bi, 0, i, 0)),
                   pl.BlockSpec((1, 1, N_HEADS * V_DIM, tm), lambda bi, i: (bi, i, 0, 0)),
                   pl.BlockSpec((1, tm, POOL_WIDTH), lambda bi, i: (bi, i, 0))),
        compiler_params=pltpu.CompilerParams(
            dimension_semantics=("arbitrary", "arbitrary"), vmem_limit_bytes=VMEM_LIMIT_BYTES),
        name="inproj_latent",
    )(x, mod, g1, wa, qg, kvg, wq, wk, wvt, cos, sin)


def _inproj_ctx(ctx, mod, ctx_row, g1, wa, kvg, wk, wvt):
    b, lc, d = ctx.shape
    return pl.pallas_call(
        _inproj_ctx_kernel,
        out_shape=(jax.ShapeDtypeStruct((b, N_HEADS, lc, HEAD_SLAB), BF16),
                   jax.ShapeDtypeStruct((b, 1, N_HEADS * V_DIM, lc), BF16)),
        grid=(b,),
        in_specs=[pl.BlockSpec((1, lc, d), lambda bi: (bi, 0, 0)),
                  pl.BlockSpec((1, 6, d), lambda bi: (ctx_row, 0, 0)),
                  _const_spec(g1.shape), _const_spec(wa.shape), _const_spec(kvg.shape),
                  _const_spec(wk.shape), _const_spec(wvt.shape)],
        out_specs=(pl.BlockSpec((1, N_HEADS, lc, HEAD_SLAB), lambda bi: (bi, 0, 0, 0)),
                   pl.BlockSpec((1, 1, N_HEADS * V_DIM, lc), lambda bi: (bi, 0, 0, 0))),
        compiler_params=pltpu.CompilerParams(
            dimension_semantics=("arbitrary",), vmem_limit_bytes=VMEM_LIMIT_BYTES),
        name="inproj_ctx",
    )(ctx, mod, g1, wa, kvg, wk, wvt)


HEADS_PER_STEP = 2
EXP2_SCALE = ATTN_SCALE * math.log2(math.e)


def _attn_kernel(q_ref, k_ref, vt_ref, kc_ref, vct_ref, o_ref, *, n_kv_blocks, kv_block):
    tq = q_ref.shape[2]
    outs = []
    for hh in range(HEADS_PER_STEP):
        q = q_ref[0, hh]
        vrows = slice(hh * V_DIM, (hh + 1) * V_DIM)

        def update(carry, k_blk, vt_blk):
            m, l, acc = carry
            s = _dot_nt(k_blk, q)
            m_new = jnp.maximum(m, jnp.max(s, axis=0, keepdims=True))
            alpha = jnp.exp2((m - m_new) * EXP2_SCALE)
            p = jnp.exp2((s - m_new) * EXP2_SCALE)
            l_new = alpha * l + jnp.sum(p, axis=0, keepdims=True)
            acc_new = alpha * acc + _dot(vt_blk, p.astype(BF16))
            return m_new, l_new, acc_new

        carry = (jnp.full((1, tq), -jnp.inf, F32), jnp.zeros((1, tq), F32),
                 jnp.zeros((V_DIM, tq), F32))
        carry = update(carry, kc_ref[0, hh], vct_ref[0, 0, vrows, :])

        def body(j, carry):
            start = pl.multiple_of(j * kv_block, kv_block)
            return update(carry, k_ref[0, hh, pl.ds(start, kv_block), :], vt_ref[0, j, vrows, :])

        m, l, acc = lax.fori_loop(0, n_kv_blocks, body, carry)
        outs.append(acc / l)
    o = jnp.concatenate(outs, axis=0)
    o_ref[0] = o.T.astype(BF16)


def _attention(q, k, vt, kc, vct, tq):
    b, nh, l, _ = q.shape
    lc = kc.shape[2]
    n_kv_blocks, kv_block = vt.shape[1], vt.shape[3]
    hp = HEADS_PER_STEP
    lanes = hp * V_DIM
    kern = functools.partial(_attn_kernel, n_kv_blocks=n_kv_blocks, kv_block=kv_block)
    return pl.pallas_call(
        kern,
        out_shape=jax.ShapeDtypeStruct((b, l, nh * V_DIM), BF16),
        grid=(b, nh // hp, l // tq),
        in_specs=[pl.BlockSpec((1, hp, tq, HEAD_SLAB), lambda bi, g, i: (bi, g, i, 0)),
                  pl.BlockSpec((1, hp, l, HEAD_SLAB), lambda bi, g, i: (bi, g, 0, 0)),
                  pl.BlockSpec((1, n_kv_blocks, lanes, kv_block), lambda bi, g, i: (bi, 0, g, 0)),
                  pl.BlockSpec((1, hp, lc, HEAD_SLAB), lambda bi, g, i: (bi, g, 0, 0)),
                  pl.BlockSpec((1, 1, lanes, lc), lambda bi, g, i: (bi, 0, g, 0))],
        out_specs=pl.BlockSpec((1, tq, lanes), lambda bi, g, i: (bi, i, g)),
        compiler_params=pltpu.CompilerParams(
            dimension_semantics=("arbitrary", "arbitrary", "arbitrary"),
            vmem_limit_bytes=VMEM_LIMIT_BYTES),
        name="attention",
    )(q, k, vt, kc, vct)


def _merge_kernel(x_ref, mod_ref, g1_ref, attn_ref, pool_ref, prev_ref, next_ref, wg_ref, wbm_ref,
                  pw_ref, ps_ref, wbp_ref, wo_ref, o_ref, ext_ref, *, seq_len):
    tm = x_ref.shape[1]
    i = pl.program_id(1)
    n_tiles = pl.num_programs(1)
    x = x_ref[0]
    mod = mod_ref[0]
    h = _modulated(x, g1_ref[...], mod[0:1], mod[1:2]).astype(BF16)
    gates = jax.nn.sigmoid(_dot(h, wg_ref[...]))

    halo = POOL_HALO
    ext_ref[0:halo, :] = jnp.where(i > 0, prev_ref[0], 0.0)
    ext_ref[halo:halo + tm, :] = pool_ref[0]
    ext_ref[halo + tm:halo + tm + halo, :] = jnp.where(i < n_tiles - 1, next_ref[0], 0.0)

    t = i * tm + lax.broadcasted_iota(jnp.int32, (tm, POOL_GC), 0)
    pooled = []
    for gi, w in enumerate(POOL_WINDOWS):
        cols = slice(gi * POOL_GC, (gi + 1) * POOL_GC)
        acc = ext_ref[halo - w // 2:halo - w // 2 + tm, cols]
        for dlt in range(1, w):
            off = halo - w // 2 + dlt
            acc = acc + ext_ref[off:off + tm, cols]
        cnt = (jnp.clip(t + w // 2, 0, seq_len) - jnp.clip(t - w // 2, 0, seq_len)).astype(F32)
        dg = (acc / cnt - ext_ref[halo:halo + tm, cols]).astype(BF16)
        pooled.append(_dot(dg, pw_ref[gi]))
    pooled = (jnp.concatenate(pooled, axis=1) * ps_ref[...]).astype(BF16)

    d = D_MODEL
    merged = (gates[:, :d] * _dot(attn_ref[0], wbm_ref[...])
              + gates[:, d:] * _dot(pooled, wbp_ref[...]))
    y = _dot(merged.astype(BF16), wo_ref[...])
    o_ref[0] = x + mod[2:3] * y


def _merge(x, mod, g1, attn, pool_in, wg, wbm, pw, ps, wbp, wo, tm):
    b, l, d = x.shape
    nt = l // tm
    hb = tm // POOL_HALO
    last_hb = l // POOL_HALO - 1
    kern = functools.partial(_merge_kernel, seq_len=l)
    return pl.pallas_call(
        kern,
        out_shape=jax.ShapeDtypeStruct((b, l, d), F32),
        grid=(b, nt),
        in_specs=[pl.BlockSpec((1, tm, d), lambda bi, i: (bi, i, 0)),
                  pl.BlockSpec((1, 6, d), lambda bi, i: (bi, 0, 0)),
                  _const_spec(g1.shape),
                  pl.BlockSpec((1, tm, attn.shape[2]), lambda bi, i: (bi, i, 0)),
                  pl.BlockSpec((1, tm, POOL_WIDTH), lambda bi, i: (bi, i, 0)),
                  pl.BlockSpec((1, POOL_HALO, POOL_WIDTH),
                               lambda bi, i: (bi, jnp.maximum(i * hb - 1, 0), 0)),
                  pl.BlockSpec((1, POOL_HALO, POOL_WIDTH),
                               lambda bi, i: (bi, jnp.minimum((i + 1) * hb, last_hb), 0)),
                  _const_spec(wg.shape), _const_spec(wbm.shape), _const_spec(pw.shape),
                  _const_spec(ps.shape), _const_spec(wbp.shape), _const_spec(wo.shape)],
        out_specs=pl.BlockSpec((1, tm, d), lambda bi, i: (bi, i, 0)),
        scratch_shapes=[pltpu.VMEM((tm + 2 * POOL_HALO, POOL_WIDTH), F32)],
        compiler_params=pltpu.CompilerParams(
            dimension_semantics=("arbitrary", "arbitrary"), vmem_limit_bytes=VMEM_LIMIT_BYTES),
        name="merge",
    )(x, mod, g1, attn, pool_in, pool_in, pool_in, wg, wbm, pw, ps, wbp, wo)


FF_CHUNK = 1024


def _mlp_kernel(x_ref, mod_ref, g2_ref, w1_ref, w2_ref, gf_ref, o_ref):
    x = x_ref[0]
    mod = mod_ref[0]
    h = _modulated(x, g2_ref[...], mod[3:4], mod[4:5]).astype(BF16)
    acc = jnp.zeros(x.shape, F32)
    for c in range(D_FF // FF_CHUNK):
        cols = slice(c * FF_CHUNK, (c + 1) * FF_CHUNK)
        hid = jnp.maximum(_dot(h, w1_ref[:, cols]), 0.0)
        acc = acc + _dot((hid * hid).astype(BF16), w2_ref[cols, :])
    x2 = x + mod[5:6] * acc
    o_ref[0] = _rms(x2, gf_ref[...])


def _mlp(x, mod, g2, w1, w2, gf, tm):
    b, l, d = x.shape
    return pl.pallas_call(
        _mlp_kernel,
        out_shape=jax.ShapeDtypeStruct((b, l, d), F32),
        grid=(b, l // tm),
        in_specs=[pl.BlockSpec((1, tm, d), lambda bi, i: (bi, i, 0)),
                  pl.BlockSpec((1, 6, d), lambda bi, i: (bi, 0, 0)),
                  _const_spec(g2.shape), _const_spec(w1.shape), _const_spec(w2.shape),
                  _const_spec(gf.shape)],
        out_specs=pl.BlockSpec((1, tm, d), lambda bi, i: (bi, i, 0)),
        compiler_params=pltpu.CompilerParams(
            dimension_semantics=("arbitrary", "arbitrary"), vmem_limit_bytes=VMEM_LIMIT_BYTES),
        name="mlp",
    )(x, mod, g2, w1, w2, gf)


def _rope_swap_perm():
    half = ROPE_AXIS // 2
    idx = []
    for base in (0, ROPE_AXIS):
        idx += list(range(base + half, base + ROPE_AXIS)) + list(range(base, base + half))
    return np.asarray(idx)


def _rope_tables(seq_len):
    t = jnp.arange(seq_len)
    row = (t // GRID_W).astype(F32)
    col = (t % GRID_W).astype(F32)
    inv_freq = ROPE_THETA ** (-jnp.arange(0, ROPE_AXIS, 2, dtype=F32) / ROPE_AXIS)
    ang_row = row[:, None] * inv_freq
    ang_col = col[:, None] * inv_freq
    ones = jnp.ones((seq_len, QK_NOPE), F32)
    pad = jnp.zeros((seq_len, HEAD_SLAB - QK_NOPE - QK_ROPE), F32)
    cos_t = jnp.concatenate([ones, jnp.cos(ang_row), jnp.cos(ang_row),
                             jnp.cos(ang_col), jnp.cos(ang_col), pad + 1.0], axis=1)
    sin_t = jnp.concatenate([0.0 * ones, -jnp.sin(ang_row), jnp.sin(ang_row),
                             -jnp.sin(ang_col), jnp.sin(ang_col), pad], axis=1)
    return cos_t, sin_t


def _pack_weights(w_in, w_uq, w_ukv):
    perm = _rope_swap_perm()
    d = w_in.shape[0]
    s0, s1, s2, s3, s4 = (Q_LORA, Q_LORA + KV_LORA, Q_LORA + KV_LORA + QK_ROPE,
                          Q_LORA + KV_LORA + QK_ROPE + POOL_WIDTH,
                          Q_LORA + KV_LORA + QK_ROPE + POOL_WIDTH + D_MODEL)
    w_kr = w_in[:, s1:s2]
    zl = jnp.zeros((d, QK_NOPE), w_in.dtype)
    zr = jnp.zeros((d, HEAD_SLAB - QK_NOPE - QK_ROPE), w_in.dtype)
    wa = jnp.concatenate([w_in[:, :s1], zl, w_kr, zr, zl, w_kr[:, perm], zr, w_in[:, s2:s3]], axis=1)
    wg = w_in[:, s3:]

    dq = QK_NOPE + QK_ROPE
    wq3 = w_uq.reshape(Q_LORA, N_HEADS, dq)
    zq = jnp.zeros((Q_LORA, N_HEADS, HEAD_SLAB - dq), w_uq.dtype)
    wq_plain = jnp.concatenate([wq3, zq], axis=2).reshape(Q_LORA, N_HEADS * HEAD_SLAB)
    wq_swap = jnp.concatenate([jnp.zeros((Q_LORA, N_HEADS, QK_NOPE), w_uq.dtype),
                               wq3[:, :, QK_NOPE:][:, :, perm], zq], axis=2
                              ).reshape(Q_LORA, N_HEADS * HEAD_SLAB)
    wq = jnp.concatenate([wq_plain, wq_swap], axis=1)

    wkv3 = w_ukv.reshape(KV_LORA, N_HEADS, QK_NOPE + V_DIM)
    wk = jnp.concatenate([wkv3[:, :, :QK_NOPE],
                          jnp.zeros((KV_LORA, N_HEADS, HEAD_SLAB - QK_NOPE), w_ukv.dtype)],
                         axis=2).reshape(KV_LORA, N_HEADS * HEAD_SLAB)
    wvt = wkv3[:, :, QK_NOPE:].reshape(KV_LORA, N_HEADS * V_DIM).T
    return wa.astype(BF16), wg.astype(BF16), wq.astype(BF16), wk.astype(BF16), wvt.astype(BF16)


TOKEN_TILE = 512
Q_TILE = 512


def kernel(x, c, ctx, c_ctx, w_ada, b_ada, norm1_g, w_in, q_norm_g, kv_norm_g, w_uq, w_ukv,
           w_br_mla, pool_w, pool_scale, w_br_pool, w_out, norm2_g, w_mlp1, w_mlp2, final_g):
    b, l, d = x.shape
    assert w_ada.shape[0] == 1, "single-layer block"
    assert l % TOKEN_TILE == 0 and l % Q_TILE == 0 and l % GRID_W == 0

    cond_rows = 16
    cond = jnp.concatenate([c, c_ctx[None, :], jnp.zeros((cond_rows - b - 1, d), F32)], axis=0)
    mod = _adaln(cond, w_ada[0], b_ada[0][None, :]).reshape(cond_rows, 6, d)

    wa, wg, wq, wk, wvt = _pack_weights(w_in[0], w_uq[0], w_ukv[0])
    cos_t, sin_t = _rope_tables(l)
    g1 = norm1_g[0][None, :]
    qg = q_norm_g[0][None, :]
    kvg = kv_norm_g[0][None, :]

    kc, vct = _inproj_ctx(ctx, mod, b, g1, wa, kvg, wk, wvt)
    q, k, vt, pool_in = _inproj_latent(x, mod, g1, wa, qg, kvg, wq, wk, wvt, cos_t, sin_t,
                                       TOKEN_TILE)
    attn = _attention(q, k, vt, kc, vct, Q_TILE)
    x1 = _merge(x, mod, g1, attn, pool_in, wg, w_br_mla[0].astype(BF16), pool_w[0].astype(BF16),
                pool_scale[0][None, :], w_br_pool[0].astype(BF16), w_out[0].astype(BF16),
                TOKEN_TILE)
    return _mlp(x1, mod, norm2_g[0][None, :], w_mlp1[0].astype(BF16), w_mlp2[0].astype(BF16),
                final_g[None, :], TOKEN_TILE)
```

```python
import functools
import math

import jax
import jax.numpy as jnp
import numpy as np
from jax import lax
from jax.experimental import pallas as pl
from jax.experimental.pallas import tpu as pltpu

D_MODEL = 1024
GRID_W = 64
N_HEADS = 8
QK_NOPE = 64
QK_ROPE = 32
ROPE_AXIS = QK_ROPE // 2
V_DIM = 64
Q_LORA = 384
KV_LORA = 256
POOL_WINDOWS = (2, 4, 8, 16)
POOL_WIDTH = 512
POOL_GC = POOL_WIDTH // len(POOL_WINDOWS)
D_FF = 4 * D_MODEL
ROPE_THETA = 10000.0
NORM_EPS = 1e-6
ATTN_SCALE = (QK_NOPE + QK_ROPE) ** -0.5

HEAD_SLAB = 128
POOL_HALO = max(POOL_WINDOWS) // 2
VMEM_LIMIT_BYTES = 56 * 1024 * 1024

BF16 = jnp.bfloat16
F32 = jnp.float32


def _dot(a, b):
    return jnp.dot(a, b, preferred_element_type=F32)


def _dot_nt(a, b):
    return lax.dot_general(a, b, (((1,), (1,)), ((), ())), preferred_element_type=F32)


def _rms(x, g):
    ms = jnp.mean(x * x, axis=-1, keepdims=True)
    return x * lax.rsqrt(ms + NORM_EPS) * g


def _modulated(x, g, shift, scale):
    return _rms(x, g) * (1.0 + scale) + shift


def _adaln_kernel(cond_ref, w_ref, b_ref, o_ref):
    c = cond_ref[...]
    act = (c * jax.nn.sigmoid(c)).astype(BF16)
    o_ref[...] = _dot(act, w_ref[...].astype(BF16)) + b_ref[...]


def _adaln(cond, w_ada, b_ada):
    rows, d = cond.shape
    n = w_ada.shape[1]
    tn = 1536
    return pl.pallas_call(
        _adaln_kernel,
        out_shape=jax.ShapeDtypeStruct((rows, n), F32),
        grid=(n // tn,),
        in_specs=[pl.BlockSpec((rows, d), lambda j: (0, 0)),
                  pl.BlockSpec((d, tn), lambda j: (0, j)),
                  pl.BlockSpec((1, tn), lambda j: (0, j))],
        out_specs=pl.BlockSpec((rows, tn), lambda j: (0, j)),
        compiler_params=pltpu.CompilerParams(
            dimension_semantics=("arbitrary",), vmem_limit_bytes=VMEM_LIMIT_BYTES),
        name="adaln",
    )(cond, w_ada, b_ada)


_A_CQ = (0, Q_LORA)
_A_CKV = (Q_LORA, Q_LORA + KV_LORA)
_A_KR = (_A_CKV[1], _A_CKV[1] + HEAD_SLAB)
_A_KRS = (_A_KR[1], _A_KR[1] + HEAD_SLAB)
_A_POOL = (_A_KRS[1], _A_KRS[1] + POOL_WIDTH)
_A_WIDTH = _A_POOL[1]


def _inproj_latent_kernel(x_ref, mod_ref, g1_ref, wa_ref, qg_ref, kvg_ref, wq_ref, wk_ref, wvt_ref,
                          cos_ref, sin_ref, k_in_ref, vt_in_ref, q_ref, k_ref, vt_ref, pool_ref):
    del k_in_ref, vt_in_ref
    x = x_ref[0]
    mod = mod_ref[0]
    h = _modulated(x, g1_ref[...], mod[0:1], mod[1:2]).astype(BF16)
    a = _dot(h, wa_ref[...])
    pool_ref[0] = a[:, _A_POOL[0]:_A_POOL[1]]
    cos = cos_ref[...]
    sin = sin_ref[...]

    cqn = _rms(a[:, _A_CQ[0]:_A_CQ[1]], qg_ref[...]).astype(BF16)
    qq = _dot(cqn, wq_ref[...])
    hw = N_HEADS * HEAD_SLAB
    for hd in range(N_HEADS):
        lo = hd * HEAD_SLAB
        qh = qq[:, lo:lo + HEAD_SLAB] * cos + qq[:, hw + lo:hw + lo + HEAD_SLAB] * sin
        q_ref[0, hd] = qh.astype(BF16)

    ckvn = _rms(a[:, _A_CKV[0]:_A_CKV[1]], kvg_ref[...]).astype(BF16)
    kk = _dot(ckvn, wk_ref[...])
    kr = a[:, _A_KR[0]:_A_KR[1]] * cos + a[:, _A_KRS[0]:_A_KRS[1]] * sin
    for hd in range(N_HEADS):
        lo = hd * HEAD_SLAB
        k_ref[0, hd] = (kk[:, lo:lo + HEAD_SLAB] + kr).astype(BF16)
    vt_ref[0] = _dot_nt(wvt_ref[...], ckvn).astype(BF16)


def _inproj_ctx_kernel(x_ref, mod_ref, g1_ref, wa_ref, kvg_ref, wk_ref, wvt_ref, k_ref, vt_ref):
    x = x_ref[0]
    mod = mod_ref[0]
    h = _modulated(x, g1_ref[...], mod[0:1], mod[1:2]).astype(BF16)
    a = _dot(h, wa_ref[:, _A_CKV[0]:_A_KR[1]])
    ckvn = _rms(a[:, :KV_LORA], kvg_ref[...]).astype(BF16)
    kk = _dot(ckvn, wk_ref[...])
    kr = a[:, KV_LORA:KV_LORA + HEAD_SLAB]
    for hd in range(N_HEADS):
        lo = hd * HEAD_SLAB
        k_ref[0, hd] = (kk[:, lo:lo + HEAD_SLAB] + kr).astype(BF16)
    vt_ref[0] = _dot_nt(wvt_ref[...], ckvn).astype(BF16)


def _const_spec(shape):
    nd = len(shape)
    return pl.BlockSpec(shape, lambda *_: (0,) * nd)


def _inproj_ctx(ctx, mod, ctx_row, g1, wa, kvg, wk, wvt, seq_len):
    b, lc, d = ctx.shape
    assert seq_len % lc == 0
    n_lat = seq_len // lc
    return pl.pallas_call(
        _inproj_ctx_kernel,
        out_shape=(jax.ShapeDtypeStruct((b, N_HEADS, seq_len + lc, HEAD_SLAB), BF16),
                   jax.ShapeDtypeStruct((b, N_HEADS * V_DIM, seq_len + lc), BF16)),
        grid=(b,),
        in_specs=[pl.BlockSpec((1, lc, d), lambda bi: (bi, 0, 0)),
                  pl.BlockSpec((1, 6, d), lambda bi: (ctx_row, 0, 0)),
                  _const_spec(g1.shape), _const_spec(wa.shape), _const_spec(kvg.shape),
                  _const_spec(wk.shape), _const_spec(wvt.shape)],
        out_specs=(pl.BlockSpec((1, N_HEADS, lc, HEAD_SLAB), lambda bi: (bi, 0, n_lat, 0)),
                   pl.BlockSpec((1, N_HEADS * V_DIM, lc), lambda bi: (bi, 0, n_lat))),
        compiler_params=pltpu.CompilerParams(
            dimension_semantics=("arbitrary",), vmem_limit_bytes=VMEM_LIMIT_BYTES),
        name="inproj_ctx",
    )(ctx, mod, g1, wa, kvg, wk, wvt)


def _inproj_latent(x, mod, g1, wa, qg, kvg, wq, wk, wvt, cos, sin, k_all, vt_all, tm):
    b, l, d = x.shape
    nt = l // tm
    any_spec = pl.BlockSpec(memory_space=pl.ANY)
    return pl.pallas_call(
        _inproj_latent_kernel,
        out_shape=(jax.ShapeDtypeStruct((b, N_HEADS, l, HEAD_SLAB), BF16),
                   jax.ShapeDtypeStruct(k_all.shape, k_all.dtype),
                   jax.ShapeDtypeStruct(vt_all.shape, vt_all.dtype),
                   jax.ShapeDtypeStruct((b, l, POOL_WIDTH), F32)),
        grid=(b, nt),
        in_specs=[pl.BlockSpec((1, tm, d), lambda bi, i: (bi, i, 0)),
                  pl.BlockSpec((1, 6, d), lambda bi, i: (bi, 0, 0)),
                  _const_spec(g1.shape), _const_spec(wa.shape), _const_spec(qg.shape),
                  _const_spec(kvg.shape), _const_spec(wq.shape), _const_spec(wk.shape),
                  _const_spec(wvt.shape),
                  pl.BlockSpec((tm, HEAD_SLAB), lambda bi, i: (i, 0)),
                  pl.BlockSpec((tm, HEAD_SLAB), lambda bi, i: (i, 0)),
                  any_spec, any_spec],
        out_specs=(pl.BlockSpec((1, N_HEADS, tm, HEAD_SLAB), lambda bi, i: (bi, 0, i, 0)),
                   pl.BlockSpec((1, N_HEADS, tm, HEAD_SLAB), lambda bi, i: (bi, 0, i, 0)),
                   pl.BlockSpec((1, N_HEADS * V_DIM, tm), lambda bi, i: (bi, 0, i)),
                   pl.BlockSpec((1, tm, POOL_WIDTH), lambda bi, i: (bi, i, 0))),
        input_output_aliases={11: 1, 12: 2},
        compiler_params=pltpu.CompilerParams(
            dimension_semantics=("arbitrary", "arbitrary"), vmem_limit_bytes=VMEM_LIMIT_BYTES),
        name="inproj_latent",
    )(x, mod, g1, wa, qg, kvg, wq, wk, wvt, cos, sin, k_all, vt_all)


HEADS_PER_STEP = 2
EXP2_SCALE = ATTN_SCALE * math.log2(math.e)
ATTN_KEY_CHUNK = 768
S_RING = 3
P_RING = 2


def _attn_kernel(q_ref, k_ref, vt_ref, o_ref, *scratch):
    s_refs = scratch[:S_RING]
    p_refs = scratch[S_RING:S_RING + P_RING]
    acc_ref, out_ref = scratch[S_RING + P_RING:]
    tq = q_ref.shape[2]
    kc = ATTN_KEY_CHUNK
    n_chunks = k_ref.shape[2] // kc

    def one_head(hh, _):
        vrow0 = pl.multiple_of(hh * V_DIM, V_DIM)

        def scores(s_ref, c):
            s = _dot_nt(k_ref[0, hh, c * kc:(c + 1) * kc, :], q_ref[0, hh])
            s_ref[...] = s
            return jnp.max(s, axis=0, keepdims=True)

        def values(p_ref, c):
            return _dot(vt_ref[0, pl.ds(vrow0, V_DIM), c * kc:(c + 1) * kc], p_ref[...])

        m = jnp.full((1, tq), -jnp.inf, F32)
        l = jnp.zeros((1, tq), F32)
        cmax = scores(s_refs[0], 0)
        for c in range(n_chunks):
            if c + 1 < n_chunks:
                cmax_nxt = scores(s_refs[(c + 1) % S_RING], c + 1)
            if c > 0:
                pv_prev = values(p_refs[(c - 1) % P_RING], c - 1)
            m_new = jnp.maximum(m, cmax)
            alpha = jnp.exp2((m - m_new) * EXP2_SCALE)
            p = jnp.exp2((s_refs[c % S_RING][...] - m_new) * EXP2_SCALE)
            l = alpha * l + jnp.sum(p, axis=0, keepdims=True)
            p_refs[c % P_RING][...] = p.astype(BF16)
            if c == 1:
                acc_ref[...] = pv_prev * alpha
            elif c > 1:
                acc_ref[...] = (acc_ref[...] + pv_prev) * alpha
            m = m_new
            if c + 1 < n_chunks:
                cmax = cmax_nxt
        last = n_chunks - 1
        out_ref[hh] = (acc_ref[...] + values(p_refs[last % P_RING], last)) / l
        return 0

    lax.fori_loop(0, HEADS_PER_STEP, one_head, 0)
    o = out_ref[...].reshape(HEADS_PER_STEP * V_DIM, tq)
    o_ref[0] = o.T.astype(BF16)


def _attention(q, k_all, vt_all, tq):
    b, nh, l, _ = q.shape
    lk = k_all.shape[2]
    assert lk % ATTN_KEY_CHUNK == 0 and lk // ATTN_KEY_CHUNK >= 2
    hp = HEADS_PER_STEP
    lanes = hp * V_DIM
    return pl.pallas_call(
        _attn_kernel,
        out_shape=jax.ShapeDtypeStruct((b, l, nh * V_DIM), BF16),
        grid=(b, nh // hp, l // tq),
        in_specs=[pl.BlockSpec((1, hp, tq, HEAD_SLAB), lambda bi, g, i: (bi, g, i, 0)),
                  pl.BlockSpec((1, hp, lk, HEAD_SLAB), lambda bi, g, i: (bi, g, 0, 0)),
                  pl.BlockSpec((1, lanes, lk), lambda bi, g, i: (bi, g, 0))],
        out_specs=pl.BlockSpec((1, tq, lanes), lambda bi, g, i: (bi, i, g)),
        scratch_shapes=([pltpu.VMEM((ATTN_KEY_CHUNK, tq), F32)] * S_RING
                        + [pltpu.VMEM((ATTN_KEY_CHUNK, tq), BF16)] * P_RING
                        + [pltpu.VMEM((V_DIM, tq), F32), pltpu.VMEM((hp, V_DIM, tq), F32)]),
        compiler_params=pltpu.CompilerParams(
            dimension_semantics=("arbitrary", "arbitrary", "arbitrary"),
            vmem_limit_bytes=VMEM_LIMIT_BYTES),
        name="attention",
    )(q, k_all, vt_all)


def _merge_kernel(x_ref, mod_ref, g1_ref, attn_ref, pool_ref, prev_ref, next_ref, wg_ref, wbm_ref,
                  pw_ref, ps_ref, wbp_ref, wo_ref, o_ref, ext_ref, *, seq_len):
    tm = x_ref.shape[1]
    i = pl.program_id(1)
    n_tiles = pl.num_programs(1)
    x = x_ref[0]
    mod = mod_ref[0]
    h = _modulated(x, g1_ref[...], mod[0:1], mod[1:2]).astype(BF16)
    gates = jax.nn.sigmoid(_dot(h, wg_ref[...]))

    halo = POOL_HALO
    ext_ref[0:halo, :] = jnp.where(i > 0, prev_ref[0], 0.0)
    ext_ref[halo:halo + tm, :] = pool_ref[0]
    ext_ref[halo + tm:halo + tm + halo, :] = jnp.where(i < n_tiles - 1, next_ref[0], 0.0)

    t = i * tm + lax.broadcasted_iota(jnp.int32, (tm, POOL_GC), 0)
    pooled = []
    for gi, w in enumerate(POOL_WINDOWS):
        cols = slice(gi * POOL_GC, (gi + 1) * POOL_GC)
        acc = ext_ref[halo - w // 2:halo - w // 2 + tm, cols]
        for dlt in range(1, w):
            off = halo - w // 2 + dlt
            acc = acc + ext_ref[off:off + tm, cols]
        cnt = (jnp.clip(t + w // 2, 0, seq_len) - jnp.clip(t - w // 2, 0, seq_len)).astype(F32)
        dg = (acc / cnt - ext_ref[halo:halo + tm, cols]).astype(BF16)
        pooled.append(_dot(dg, pw_ref[gi]))
    pooled = (jnp.concatenate(pooled, axis=1) * ps_ref[...]).astype(BF16)

    d = D_MODEL
    merged = (gates[:, :d] * _dot(attn_ref[0], wbm_ref[...])
              + gates[:, d:] * _dot(pooled, wbp_ref[...]))
    y = _dot(merged.astype(BF16), wo_ref[...])
    o_ref[0] = x + mod[2:3] * y


def _merge(x, mod, g1, attn, pool_in, wg, wbm, pw, ps, wbp, wo, tm):
    b, l, d = x.shape
    nt = l // tm
    hb = tm // POOL_HALO
    last_hb = l // POOL_HALO - 1
    kern = functools.partial(_merge_kernel, seq_len=l)
    return pl.pallas_call(
        kern,
        out_shape=jax.ShapeDtypeStruct((b, l, d), F32),
        grid=(b, nt),
        in_specs=[pl.BlockSpec((1, tm, d), lambda bi, i: (bi, i, 0)),
                  pl.BlockSpec((1, 6, d), lambda bi, i: (bi, 0, 0)),
                  _const_spec(g1.shape),
                  pl.BlockSpec((1, tm, attn.shape[2]), lambda bi, i: (bi, i, 0)),
                  pl.BlockSpec((1, tm, POOL_WIDTH), lambda bi, i: (bi, i, 0)),
                  pl.BlockSpec((1, POOL_HALO, POOL_WIDTH),
                               lambda bi, i: (bi, jnp.maximum(i * hb - 1, 0), 0)),
                  pl.BlockSpec((1, POOL_HALO, POOL_WIDTH),
                               lambda bi, i: (bi, jnp.minimum((i + 1) * hb, last_hb), 0)),
                  _const_spec(wg.shape), _const_spec(wbm.shape), _const_spec(pw.shape),
                  _const_spec(ps.shape), _const_spec(wbp.shape), _const_spec(wo.shape)],
        out_specs=pl.BlockSpec((1, tm, d), lambda bi, i: (bi, i, 0)),
        scratch_shapes=[pltpu.VMEM((tm + 2 * POOL_HALO, POOL_WIDTH), F32)],
        compiler_params=pltpu.CompilerParams(
            dimension_semantics=("arbitrary", "arbitrary"), vmem_limit_bytes=VMEM_LIMIT_BYTES),
        name="merge",
    )(x, mod, g1, attn, pool_in, pool_in, pool_in, wg, wbm, pw, ps, wbp, wo)


FF_CHUNK = 1024


def _mlp_kernel(x_ref, mod_ref, g2_ref, w1_ref, w2_ref, gf_ref, o_ref):
    x = x_ref[0]
    mod = mod_ref[0]
    h = _modulated(x, g2_ref[...], mod[3:4], mod[4:5]).astype(BF16)
    acc = jnp.zeros(x.shape, F32)
    for c in range(D_FF // FF_CHUNK):
        cols = slice(c * FF_CHUNK, (c + 1) * FF_CHUNK)
        hid = jnp.maximum(_dot(h, w1_ref[:, cols]), 0.0)
        acc = acc + _dot((hid * hid).astype(BF16), w2_ref[cols, :])
    x2 = x + mod[5:6] * acc
    o_ref[0] = _rms(x2, gf_ref[...])


def _mlp(x, mod, g2, w1, w2, gf, tm):
    b, l, d = x.shape
    return pl.pallas_call(
        _mlp_kernel,
        out_shape=jax.ShapeDtypeStruct((b, l, d), F32),
        grid=(b, l // tm),
        in_specs=[pl.BlockSpec((1, tm, d), lambda bi, i: (bi, i, 0)),
                  pl.BlockSpec((1, 6, d), lambda bi, i: (bi, 0, 0)),
                  _const_spec(g2.shape), _const_spec(w1.shape), _const_spec(w2.shape),
                  _const_spec(gf.shape)],
        out_specs=pl.BlockSpec((1, tm, d), lambda bi, i: (bi, i, 0)),
        compiler_params=pltpu.CompilerParams(
            dimension_semantics=("arbitrary", "arbitrary"), vmem_limit_bytes=VMEM_LIMIT_BYTES),
        name="mlp",
    )(x, mod, g2, w1, w2, gf)


def _rope_swap_perm():
    half = ROPE_AXIS // 2
    idx = []
    for base in (0, ROPE_AXIS):
        idx += list(range(base + half, base + ROPE_AXIS)) + list(range(base, base + half))
    return np.asarray(idx)


def _rope_tables(seq_len):
    t = jnp.arange(seq_len)
    row = (t // GRID_W).astype(F32)
    col = (t % GRID_W).astype(F32)
    inv_freq = ROPE_THETA ** (-jnp.arange(0, ROPE_AXIS, 2, dtype=F32) / ROPE_AXIS)
    ang_row = row[:, None] * inv_freq
    ang_col = col[:, None] * inv_freq
    ones = jnp.ones((seq_len, QK_NOPE), F32)
    pad = jnp.zeros((seq_len, HEAD_SLAB - QK_NOPE - QK_ROPE), F32)
    cos_t = jnp.concatenate([ones, jnp.cos(ang_row), jnp.cos(ang_row),
                             jnp.cos(ang_col), jnp.cos(ang_col), pad + 1.0], axis=1)
    sin_t = jnp.concatenate([0.0 * ones, -jnp.sin(ang_row), jnp.sin(ang_row),
                             -jnp.sin(ang_col), jnp.sin(ang_col), pad], axis=1)
    return cos_t, sin_t


def _pack_weights(w_in, w_uq, w_ukv):
    perm = _rope_swap_perm()
    d = w_in.shape[0]
    s0, s1, s2, s3, s4 = (Q_LORA, Q_LORA + KV_LORA, Q_LORA + KV_LORA + QK_ROPE,
                          Q_LORA + KV_LORA + QK_ROPE + POOL_WIDTH,
                          Q_LORA + KV_LORA + QK_ROPE + POOL_WIDTH + D_MODEL)
    w_kr = w_in[:, s1:s2]
    zl = jnp.zeros((d, QK_NOPE), w_in.dtype)
    zr = jnp.zeros((d, HEAD_SLAB - QK_NOPE - QK_ROPE), w_in.dtype)
    wa = jnp.concatenate([w_in[:, :s1], zl, w_kr, zr, zl, w_kr[:, perm], zr, w_in[:, s2:s3]], axis=1)
    wg = w_in[:, s3:]

    dq = QK_NOPE + QK_ROPE
    wq3 = w_uq.reshape(Q_LORA, N_HEADS, dq)
    zq = jnp.zeros((Q_LORA, N_HEADS, HEAD_SLAB - dq), w_uq.dtype)
    wq_plain = jnp.concatenate([wq3, zq], axis=2).reshape(Q_LORA, N_HEADS * HEAD_SLAB)
    wq_swap = jnp.concatenate([jnp.zeros((Q_LORA, N_HEADS, QK_NOPE), w_uq.dtype),
                               wq3[:, :, QK_NOPE:][:, :, perm], zq], axis=2
                              ).reshape(Q_LORA, N_HEADS * HEAD_SLAB)
    wq = jnp.concatenate([wq_plain, wq_swap], axis=1)

    wkv3 = w_ukv.reshape(KV_LORA, N_HEADS, QK_NOPE + V_DIM)
    wk = jnp.concatenate([wkv3[:, :, :QK_NOPE],
                          jnp.zeros((KV_LORA, N_HEADS, HEAD_SLAB - QK_NOPE), w_ukv.dtype)],
                         axis=2).reshape(KV_LORA, N_HEADS * HEAD_SLAB)
    wvt = wkv3[:, :, QK_NOPE:].reshape(KV_LORA, N_HEADS * V_DIM).T
    return wa.astype(BF16), wg.astype(BF16), wq.astype(BF16), wk.astype(BF16), wvt.astype(BF16)


TOKEN_TILE = 512
Q_TILE = 512


def kernel(x, c, ctx, c_ctx, w_ada, b_ada, norm1_g, w_in, q_norm_g, kv_norm_g, w_uq, w_ukv,
           w_br_mla, pool_w, pool_scale, w_br_pool, w_out, norm2_g, w_mlp1, w_mlp2, final_g):
    b, l, d = x.shape
    assert w_ada.shape[0] == 1, "single-layer block"
    assert l % TOKEN_TILE == 0 and l % Q_TILE == 0 and l % GRID_W == 0

    cond_rows = 16
    cond = jnp.concatenate([c, c_ctx[None, :], jnp.zeros((cond_rows - b - 1, d), F32)], axis=0)
    mod = _adaln(cond, w_ada[0], b_ada[0][None, :]).reshape(cond_rows, 6, d)

    wa, wg, wq, wk, wvt = _pack_weights(w_in[0], w_uq[0], w_ukv[0])
    cos_t, sin_t = _rope_tables(l)
    g1 = norm1_g[0][None, :]
    qg = q_norm_g[0][None, :]
    kvg = kv_norm_g[0][None, :]

    k_all, vt_all = _inproj_ctx(ctx, mod, b, g1, wa, kvg, wk, wvt, l)
    q, k_all, vt_all, pool_in = _inproj_latent(x, mod, g1, wa, qg, kvg, wq, wk, wvt, cos_t, sin_t,
                                               k_all, vt_all, TOKEN_TILE)
    attn = _attention(q, k_all, vt_all, Q_TILE)
    x1 = _merge(x, mod, g1, attn, pool_in, wg, w_br_mla[0].astype(BF16), pool_w[0].astype(BF16),
                pool_scale[0][None, :], w_br_pool[0].astype(BF16), w_out[0].astype(BF16),
                TOKEN_TILE)
    return _mlp(x1, mod, norm2_g[0][None, :], w_mlp1[0].astype(BF16), w_mlp2[0].astype(BF16),
                final_g[None, :], TOKEN_TILE)
```

```python
import functools
import math

import jax
import jax.numpy as jnp
import numpy as np
from jax import lax
from jax.experimental import pallas as pl
from jax.experimental.pallas import tpu as pltpu

D_MODEL = 1024
GRID_W = 64
N_HEADS = 8
QK_NOPE = 64
QK_ROPE = 32
ROPE_AXIS = QK_ROPE // 2
V_DIM = 64
Q_LORA = 384
KV_LORA = 256
POOL_WINDOWS = (2, 4, 8, 16)
POOL_WIDTH = 512
POOL_GC = POOL_WIDTH // len(POOL_WINDOWS)
D_FF = 4 * D_MODEL
ROPE_THETA = 10000.0
NORM_EPS = 1e-6
ATTN_SCALE = (QK_NOPE + QK_ROPE) ** -0.5

HEAD_SLAB = 128
POOL_HALO = max(POOL_WINDOWS) // 2
VMEM_LIMIT_BYTES = 56 * 1024 * 1024

BF16 = jnp.bfloat16
F32 = jnp.float32


def _dot(a, b):
    return jnp.dot(a, b, preferred_element_type=F32)


def _dot_nt(a, b):
    return lax.dot_general(a, b, (((1,), (1,)), ((), ())), preferred_element_type=F32)


def _rms(x, g):
    ms = jnp.mean(x * x, axis=-1, keepdims=True)
    return x * lax.rsqrt(ms + NORM_EPS) * g


def _modulated(x, g, shift, scale):
    return _rms(x, g) * (1.0 + scale) + shift


def _adaln_kernel(cond_ref, w_ref, b_ref, o_ref):
    c = cond_ref[...]
    act = (c * jax.nn.sigmoid(c)).astype(BF16)
    o_ref[...] = _dot(act, w_ref[...].astype(BF16)) + b_ref[...]


def _adaln(cond, w_ada, b_ada):
    rows, d = cond.shape
    n = w_ada.shape[1]
    tn = 1536
    return pl.pallas_call(
        _adaln_kernel,
        out_shape=jax.ShapeDtypeStruct((rows, n), F32),
        grid=(n // tn,),
        in_specs=[pl.BlockSpec((rows, d), lambda j: (0, 0)),
                  pl.BlockSpec((d, tn), lambda j: (0, j)),
                  pl.BlockSpec((1, tn), lambda j: (0, j))],
        out_specs=pl.BlockSpec((rows, tn), lambda j: (0, j)),
        compiler_params=pltpu.CompilerParams(
            dimension_semantics=("arbitrary",), vmem_limit_bytes=VMEM_LIMIT_BYTES),
        name="adaln",
    )(cond, w_ada, b_ada)


_A_CQ = (0, Q_LORA)
_A_CKV = (Q_LORA, Q_LORA + KV_LORA)
_A_KR = (_A_CKV[1], _A_CKV[1] + HEAD_SLAB)
_A_KRS = (_A_KR[1], _A_KR[1] + HEAD_SLAB)
_A_POOL = (_A_KRS[1], _A_KRS[1] + POOL_WIDTH)
_A_WIDTH = _A_POOL[1]


def _inproj_latent_kernel(x_ref, mod_ref, g1_ref, wa_ref, qg_ref, kvg_ref, wq_ref, wk_ref, wvt_ref,
                          cos_ref, sin_ref, q_ref, k_ref, vt_ref, pool_ref):
    x = x_ref[0]
    mod = mod_ref[0]
    h = _modulated(x, g1_ref[...], mod[0:1], mod[1:2]).astype(BF16)
    a = _dot(h, wa_ref[...])
    pool_ref[0] = a[:, _A_POOL[0]:_A_POOL[1]]
    cos = cos_ref[...]
    sin = sin_ref[...]

    cqn = _rms(a[:, _A_CQ[0]:_A_CQ[1]], qg_ref[...]).astype(BF16)
    qq = _dot(cqn, wq_ref[...])
    hw = N_HEADS * HEAD_SLAB
    for hd in range(N_HEADS):
        lo = hd * HEAD_SLAB
        qh = qq[:, lo:lo + HEAD_SLAB] * cos + qq[:, hw + lo:hw + lo + HEAD_SLAB] * sin
        q_ref[0, hd] = qh.astype(BF16)

    ckvn = _rms(a[:, _A_CKV[0]:_A_CKV[1]], kvg_ref[...]).astype(BF16)
    kk = _dot(ckvn, wk_ref[...])
    kr = a[:, _A_KR[0]:_A_KR[1]] * cos + a[:, _A_KRS[0]:_A_KRS[1]] * sin
    for hd in range(N_HEADS):
        lo = hd * HEAD_SLAB
        k_ref[0, hd] = (kk[:, lo:lo + HEAD_SLAB] + kr).astype(BF16)
    vt_ref[0] = _dot_nt(wvt_ref[...], ckvn).astype(BF16)


def _inproj_ctx_kernel(x_ref, mod_ref, g1_ref, wa_ref, kvg_ref, wk_ref, wvt_ref, k_ref, vt_ref):
    x = x_ref[0]
    mod = mod_ref[0]
    h = _modulated(x, g1_ref[...], mod[0:1], mod[1:2]).astype(BF16)
    a = _dot(h, wa_ref[:, _A_CKV[0]:_A_KR[1]])
    ckvn = _rms(a[:, :KV_LORA], kvg_ref[...]).astype(BF16)
    kk = _dot(ckvn, wk_ref[...])
    kr = a[:, KV_LORA:KV_LORA + HEAD_SLAB]
    for hd in range(N_HEADS):
        lo = hd * HEAD_SLAB
        k_ref[0, hd] = (kk[:, lo:lo + HEAD_SLAB] + kr).astype(BF16)
    vt_ref[0] = _dot_nt(wvt_ref[...], ckvn).astype(BF16)


def _const_spec(shape):
    nd = len(shape)
    return pl.BlockSpec(shape, lambda *_: (0,) * nd)


def _inproj_ctx(ctx, mod, ctx_row, g1, wa, kvg, wk, wvt):
    b, lc, d = ctx.shape
    return pl.pallas_call(
        _inproj_ctx_kernel,
        out_shape=(jax.ShapeDtypeStruct((b, N_HEADS, lc, HEAD_SLAB), BF16),
                   jax.ShapeDtypeStruct((b, N_HEADS * V_DIM, lc), BF16)),
        grid=(b,),
        in_specs=[pl.BlockSpec((1, lc, d), lambda bi: (bi, 0, 0)),
                  pl.BlockSpec((1, 6, d), lambda bi: (ctx_row, 0, 0)),
                  _const_spec(g1.shape), _const_spec(wa.shape), _const_spec(kvg.shape),
                  _const_spec(wk.shape), _const_spec(wvt.shape)],
        out_specs=(pl.BlockSpec((1, N_HEADS, lc, HEAD_SLAB), lambda bi: (bi, 0, 0, 0)),
                   pl.BlockSpec((1, N_HEADS * V_DIM, lc), lambda bi: (bi, 0, 0))),
        compiler_params=pltpu.CompilerParams(
            dimension_semantics=("arbitrary",), vmem_limit_bytes=VMEM_LIMIT_BYTES),
        name="inproj_ctx",
    )(ctx, mod, g1, wa, kvg, wk, wvt)


def _inproj_latent(x, mod, g1, wa, qg, kvg, wq, wk, wvt, cos, sin, tm):
    b, l, d = x.shape
    nt = l // tm
    return pl.pallas_call(
        _inproj_latent_kernel,
        out_shape=(jax.ShapeDtypeStruct((b, N_HEADS, l, HEAD_SLAB), BF16),
                   jax.ShapeDtypeStruct((b, N_HEADS, l, HEAD_SLAB), BF16),
                   jax.ShapeDtypeStruct((b, N_HEADS * V_DIM, l), BF16),
                   jax.ShapeDtypeStruct((b, l, POOL_WIDTH), F32)),
        grid=(b, nt),
        in_specs=[pl.BlockSpec((1, tm, d), lambda bi, i: (bi, i, 0)),
                  pl.BlockSpec((1, 6, d), lambda bi, i: (bi, 0, 0)),
                  _const_spec(g1.shape), _const_spec(wa.shape), _const_spec(qg.shape),
                  _const_spec(kvg.shape), _const_spec(wq.shape), _const_spec(wk.shape),
                  _const_spec(wvt.shape),
                  pl.BlockSpec((tm, HEAD_SLAB), lambda bi, i: (i, 0)),
                  pl.BlockSpec((tm, HEAD_SLAB), lambda bi, i: (i, 0))],
        out_specs=(pl.BlockSpec((1, N_HEADS, tm, HEAD_SLAB), lambda bi, i: (bi, 0, i, 0)),
                   pl.BlockSpec((1, N_HEADS, tm, HEAD_SLAB), lambda bi, i: (bi, 0, i, 0)),
                   pl.BlockSpec((1, N_HEADS * V_DIM, tm), lambda bi, i: (bi, 0, i)),
                   pl.BlockSpec((1, tm, POOL_WIDTH), lambda bi, i: (bi, i, 0))),
        compiler_params=pltpu.CompilerParams(
            dimension_semantics=("arbitrary", "arbitrary"), vmem_limit_bytes=VMEM_LIMIT_BYTES),
        name="inproj_latent",
    )(x, mod, g1, wa, qg, kvg, wq, wk, wvt, cos, sin)


HEADS_PER_STEP = 2
EXP2_SCALE = ATTN_SCALE * math.log2(math.e)
ATTN_KEY_CHUNK = 768
S_RING = 2
P_RING = 2


def _key_chunks(n_latent, n_ctx):
    kc = ATTN_KEY_CHUNK
    chunks = [[(False, s, min(kc, n_latent - s))] for s in range(0, n_latent, kc)]
    if chunks[-1][0][2] + n_ctx <= kc:
        chunks[-1].append((True, 0, n_ctx))
    else:
        chunks.append([(True, 0, n_ctx)])
    return chunks


def _attn_kernel(q_ref, k_ref, vt_ref, kc_ref, vct_ref, o_ref, *scratch):
    s_refs = scratch[:S_RING]
    p_refs = scratch[S_RING:S_RING + P_RING]
    acc_ref, out_ref = scratch[S_RING + P_RING:]
    tq = q_ref.shape[2]
    chunks = _key_chunks(k_ref.shape[2], kc_ref.shape[2])
    n_chunks = len(chunks)

    def one_head(hh, _):
        vrow0 = pl.multiple_of(hh * V_DIM, V_DIM)

        def scores(s_ref, chunk):
            cmax, row = None, 0
            for is_ctx, start, size in chunk:
                keys = (kc_ref if is_ctx else k_ref)[0, hh, start:start + size, :]
                s = _dot_nt(keys, q_ref[0, hh])
                s_ref[row:row + size, :] = s
                part = jnp.max(s, axis=0, keepdims=True)
                cmax = part if cmax is None else jnp.maximum(cmax, part)
                row += size
            return cmax

        def values(p_ref, chunk):
            pv, row = None, 0
            for is_ctx, start, size in chunk:
                vt_blk = (vct_ref if is_ctx else vt_ref)[0, pl.ds(vrow0, V_DIM), start:start + size]
                part = _dot(vt_blk, p_ref[row:row + size, :])
                pv = part if pv is None else pv + part
                row += size
            return pv

        m = jnp.full((1, tq), -jnp.inf, F32)
        l = jnp.zeros((1, tq), F32)
        cmax = scores(s_refs[0], chunks[0])
        for c, chunk in enumerate(chunks):
            if c + 1 < n_chunks:
                cmax_nxt = scores(s_refs[(c + 1) % S_RING], chunks[c + 1])
            if c > 0:
                pv_prev = values(p_refs[(c - 1) % P_RING], chunks[c - 1])
            rows = sum(size for _, _, size in chunk)
            m_new = jnp.maximum(m, cmax)
            alpha = jnp.exp2((m - m_new) * EXP2_SCALE)
            p = jnp.exp2((s_refs[c % S_RING][0:rows, :] - m_new) * EXP2_SCALE)
            l = alpha * l + jnp.sum(p, axis=0, keepdims=True)
            p_refs[c % P_RING][0:rows, :] = p.astype(BF16)
            if c == 1:
                acc_ref[...] = pv_prev * alpha
            elif c > 1:
                acc_ref[...] = (acc_ref[...] + pv_prev) * alpha
            m = m_new
            if c + 1 < n_chunks:
                cmax = cmax_nxt
        last = n_chunks - 1
        out_ref[hh] = (acc_ref[...] + values(p_refs[last % P_RING], chunks[last])) / l
        return 0

    lax.fori_loop(0, HEADS_PER_STEP, one_head, 0)
    o = out_ref[...].reshape(HEADS_PER_STEP * V_DIM, tq)
    o_ref[0] = o.T.astype(BF16)


def _attention(q, k, vt, kc, vct, tq):
    b, nh, l, _ = q.shape
    lc = kc.shape[2]
    assert len(_key_chunks(l, lc)) >= 2
    hp = HEADS_PER_STEP
    lanes = hp * V_DIM
    return pl.pallas_call(
        _attn_kernel,
        out_shape=jax.ShapeDtypeStruct((b, l, nh * V_DIM), BF16),
        grid=(b, nh // hp, l // tq),
        in_specs=[pl.BlockSpec((1, hp, tq, HEAD_SLAB), lambda bi, g, i: (bi, g, i, 0)),
                  pl.BlockSpec((1, hp, l, HEAD_SLAB), lambda bi, g, i: (bi, g, 0, 0)),
                  pl.BlockSpec((1, lanes, l), lambda bi, g, i: (bi, g, 0)),
                  pl.BlockSpec((1, hp, lc, HEAD_SLAB), lambda bi, g, i: (bi, g, 0, 0)),
                  pl.BlockSpec((1, lanes, lc), lambda bi, g, i: (bi, g, 0))],
        out_specs=pl.BlockSpec((1, tq, lanes), lambda bi, g, i: (bi, i, g)),
        scratch_shapes=([pltpu.VMEM((ATTN_KEY_CHUNK, tq), F32)] * S_RING
                        + [pltpu.VMEM((ATTN_KEY_CHUNK, tq), BF16)] * P_RING
                        + [pltpu.VMEM((V_DIM, tq), F32), pltpu.VMEM((hp, V_DIM, tq), F32)]),
        compiler_params=pltpu.CompilerParams(
            dimension_semantics=("arbitrary", "arbitrary", "arbitrary"),
            vmem_limit_bytes=VMEM_LIMIT_BYTES),
        name="attention",
    )(q, k, vt, kc, vct)


def _merge_kernel(x_ref, mod_ref, g1_ref, attn_ref, pool_ref, prev_ref, next_ref, wg_ref, wbm_ref,
                  pw_ref, ps_ref, wbp_ref, wo_ref, o_ref, ext_ref, *, seq_len):
    tm = x_ref.shape[1]
    i = pl.program_id(1)
    n_tiles = pl.num_programs(1)
    x = x_ref[0]
    mod = mod_ref[0]
    h = _modulated(x, g1_ref[...], mod[0:1], mod[1:2]).astype(BF16)
    gates = jax.nn.sigmoid(_dot(h, wg_ref[...]))

    halo = POOL_HALO
    ext_ref[0:halo, :] = jnp.where(i > 0, prev_ref[0], 0.0)
    ext_ref[halo:halo + tm, :] = pool_ref[0]
    ext_ref[halo + tm:halo + tm + halo, :] = jnp.where(i < n_tiles - 1, next_ref[0], 0.0)

    t = i * tm + lax.broadcasted_iota(jnp.int32, (tm, POOL_GC), 0)
    pooled = []
    for gi, w in enumerate(POOL_WINDOWS):
        cols = slice(gi * POOL_GC, (gi + 1) * POOL_GC)
        acc = ext_ref[halo - w // 2:halo - w // 2 + tm, cols]
        for dlt in range(1, w):
            off = halo - w // 2 + dlt
            acc = acc + ext_ref[off:off + tm, cols]
        cnt = (jnp.clip(t + w // 2, 0, seq_len) - jnp.clip(t - w // 2, 0, seq_len)).astype(F32)
        dg = (acc / cnt - ext_ref[halo:halo + tm, cols]).astype(BF16)
        pooled.append(_dot(dg, pw_ref[gi]))
    pooled = (jnp.concatenate(pooled, axis=1) * ps_ref[...]).astype(BF16)

    d = D_MODEL
    merged = (gates[:, :d] * _dot(attn_ref[0], wbm_ref[...])
              + gates[:, d:] * _dot(pooled, wbp_ref[...]))
    y = _dot(merged.astype(BF16), wo_ref[...])
    o_ref[0] = x + mod[2:3] * y


def _merge(x, mod, g1, attn, pool_in, wg, wbm, pw, ps, wbp, wo, tm):
    b, l, d = x.shape
    nt = l // tm
    hb = tm // POOL_HALO
    last_hb = l // POOL_HALO - 1
    kern = functools.partial(_merge_kernel, seq_len=l)
    return pl.pallas_call(
        kern,
        out_shape=jax.ShapeDtypeStruct((b, l, d), F32),
        grid=(b, nt),
        in_specs=[pl.BlockSpec((1, tm, d), lambda bi, i: (bi, i, 0)),
                  pl.BlockSpec((1, 6, d), lambda bi, i: (bi, 0, 0)),
                  _const_spec(g1.shape),
                  pl.BlockSpec((1, tm, attn.shape[2]), lambda bi, i: (bi, i, 0)),
                  pl.BlockSpec((1, tm, POOL_WIDTH), lambda bi, i: (bi, i, 0)),
                  pl.BlockSpec((1, POOL_HALO, POOL_WIDTH),
                               lambda bi, i: (bi, jnp.maximum(i * hb - 1, 0), 0)),
                  pl.BlockSpec((1, POOL_HALO, POOL_WIDTH),
                               lambda bi, i: (bi, jnp.minimum((i + 1) * hb, last_hb), 0)),
                  _const_spec(wg.shape), _const_spec(wbm.shape), _const_spec(pw.shape),
                  _const_spec(ps.shape), _const_spec(wbp.shape), _const_spec(wo.shape)],
        out_specs=pl.BlockSpec((1, tm, d), lambda bi, i: (bi, i, 0)),
        scratch_shapes=[pltpu.VMEM((tm + 2 * POOL_HALO, POOL_WIDTH), F32)],
        compiler_params=pltpu.CompilerParams(
            dimension_semantics=("arbitrary", "arbitrary"), vmem_limit_bytes=VMEM_LIMIT_BYTES),
        name="merge",
    )(x, mod, g1, attn, pool_in, pool_in, pool_in, wg, wbm, pw, ps, wbp, wo)


FF_CHUNK = 1024


def _mlp_kernel(x_ref, mod_ref, g2_ref, w1_ref, w2_ref, gf_ref, o_ref):
    x = x_ref[0]
    mod = mod_ref[0]
    h = _modulated(x, g2_ref[...], mod[3:4], mod[4:5]).astype(BF16)
    acc = jnp.zeros(x.shape, F32)
    for c in range(D_FF // FF_CHUNK):
        cols = slice(c * FF_CHUNK, (c + 1) * FF_CHUNK)
        hid = jnp.maximum(_dot(h, w1_ref[:, cols]), 0.0)
        acc = acc + _dot((hid * hid).astype(BF16), w2_ref[cols, :])
    x2 = x + mod[5:6] * acc
    o_ref[0] = _rms(x2, gf_ref[...])


def _mlp(x, mod, g2, w1, w2, gf, tm):
    b, l, d = x.shape
    return pl.pallas_call(
        _mlp_kernel,
        out_shape=jax.ShapeDtypeStruct((b, l, d), F32),
        grid=(b, l // tm),
        in_specs=[pl.BlockSpec((1, tm, d), lambda bi, i: (bi, i, 0)),
                  pl.BlockSpec((1, 6, d), lambda bi, i: (bi, 0, 0)),
                  _const_spec(g2.shape), _const_spec(w1.shape), _const_spec(w2.shape),
                  _const_spec(gf.shape)],
        out_specs=pl.BlockSpec((1, tm, d), lambda bi, i: (bi, i, 0)),
        compiler_params=pltpu.CompilerParams(
            dimension_semantics=("arbitrary", "arbitrary"), vmem_limit_bytes=VMEM_LIMIT_BYTES),
        name="mlp",
    )(x, mod, g2, w1, w2, gf)


def _rope_swap_perm():
    half = ROPE_AXIS // 2
    idx = []
    for base in (0, ROPE_AXIS):
        idx += list(range(base + half, base + ROPE_AXIS)) + list(range(base, base + half))
    return np.asarray(idx)


def _rope_tables(seq_len):
    t = jnp.arange(seq_len)
    row = (t // GRID_W).astype(F32)
    col = (t % GRID_W).astype(F32)
    inv_freq = ROPE_THETA ** (-jnp.arange(0, ROPE_AXIS, 2, dtype=F32) / ROPE_AXIS)
    ang_row = row[:, None] * inv_freq
    ang_col = col[:, None] * inv_freq
    ones = jnp.ones((seq_len, QK_NOPE), F32)
    pad = jnp.zeros((seq_len, HEAD_SLAB - QK_NOPE - QK_ROPE), F32)
    cos_t = jnp.concatenate([ones, jnp.cos(ang_row), jnp.cos(ang_row),
                             jnp.cos(ang_col), jnp.cos(ang_col), pad + 1.0], axis=1)
    sin_t = jnp.concatenate([0.0 * ones, -jnp.sin(ang_row), jnp.sin(ang_row),
                             -jnp.sin(ang_col), jnp.sin(ang_col), pad], axis=1)
    return cos_t, sin_t


def _pack_weights(w_in, w_uq, w_ukv):
    perm = _rope_swap_perm()
    d = w_in.shape[0]
    s0, s1, s2, s3, s4 = (Q_LORA, Q_LORA + KV_LORA, Q_LORA + KV_LORA + QK_ROPE,
                          Q_LORA + KV_LORA + QK_ROPE + POOL_WIDTH,
                          Q_LORA + KV_LORA + QK_ROPE + POOL_WIDTH + D_MODEL)
    w_kr = w_in[:, s1:s2]
    zl = jnp.zeros((d, QK_NOPE), w_in.dtype)
    zr = jnp.zeros((d, HEAD_SLAB - QK_NOPE - QK_ROPE), w_in.dtype)
    wa = jnp.concatenate([w_in[:, :s1], zl, w_kr, zr, zl, w_kr[:, perm], zr, w_in[:, s2:s3]], axis=1)
    wg = w_in[:, s3:]

    dq = QK_NOPE + QK_ROPE
    wq3 = w_uq.reshape(Q_LORA, N_HEADS, dq)
    zq = jnp.zeros((Q_LORA, N_HEADS, HEAD_SLAB - dq), w_uq.dtype)
    wq_plain = jnp.concatenate([wq3, zq], axis=2).reshape(Q_LORA, N_HEADS * HEAD_SLAB)
    wq_swap = jnp.concatenate([jnp.zeros((Q_LORA, N_HEADS, QK_NOPE), w_uq.dtype),
                               wq3[:, :, QK_NOPE:][:, :, perm], zq], axis=2
                              ).reshape(Q_LORA, N_HEADS * HEAD_SLAB)
    wq = jnp.concatenate([wq_plain, wq_swap], axis=1)

    wkv3 = w_ukv.reshape(KV_LORA, N_HEADS, QK_NOPE + V_DIM)
    wk = jnp.concatenate([wkv3[:, :, :QK_NOPE],
                          jnp.zeros((KV_LORA, N_HEADS, HEAD_SLAB - QK_NOPE), w_ukv.dtype)],
                         axis=2).reshape(KV_LORA, N_HEADS * HEAD_SLAB)
    wvt = wkv3[:, :, QK_NOPE:].reshape(KV_LORA, N_HEADS * V_DIM).T
    return wa.astype(BF16), wg.astype(BF16), wq.astype(BF16), wk.astype(BF16), wvt.astype(BF16)


TOKEN_TILE = 512
Q_TILE = 512


def kernel(x, c, ctx, c_ctx, w_ada, b_ada, norm1_g, w_in, q_norm_g, kv_norm_g, w_uq, w_ukv,
           w_br_mla, pool_w, pool_scale, w_br_pool, w_out, norm2_g, w_mlp1, w_mlp2, final_g):
    b, l, d = x.shape
    assert w_ada.shape[0] == 1, "single-layer block"
    assert l % TOKEN_TILE == 0 and l % Q_TILE == 0 and l % GRID_W == 0

    cond_rows = 16
    cond = jnp.concatenate([c, c_ctx[None, :], jnp.zeros((cond_rows - b - 1, d), F32)], axis=0)
    mod = _adaln(cond, w_ada[0], b_ada[0][None, :]).reshape(cond_rows, 6, d)

    wa, wg, wq, wk, wvt = _pack_weights(w_in[0], w_uq[0], w_ukv[0])
    cos_t, sin_t = _rope_tables(l)
    g1 = norm1_g[0][None, :]
    qg = q_norm_g[0][None, :]
    kvg = kv_norm_g[0][None, :]

    kc, vct = _inproj_ctx(ctx, mod, b, g1, wa, kvg, wk, wvt)
    q, k, vt, pool_in = _inproj_latent(x, mod, g1, wa, qg, kvg, wq, wk, wvt, cos_t, sin_t,
                                       TOKEN_TILE)
    attn = _attention(q, k, vt, kc, vct, Q_TILE)
    x1 = _merge(x, mod, g1, attn, pool_in, wg, w_br_mla[0].astype(BF16), pool_w[0].astype(BF16),
                pool_scale[0][None, :], w_br_pool[0].astype(BF16), w_out[0].astype(BF16),
                TOKEN_TILE)
    return _mlp(x1, mod, norm2_g[0][None, :], w_mlp1[0].astype(BF16), w_mlp2[0].astype(BF16),
                final_g[None, :], TOKEN_TILE)
```

```python
import functools
import math

import jax
import jax.numpy as jnp
import numpy as np
from jax import lax
from jax.experimental import pallas as pl
from jax.experimental.pallas import tpu as pltpu

D_MODEL = 1024
GRID_W = 64
N_HEADS = 8
QK_NOPE = 64
QK_ROPE = 32
ROPE_AXIS = QK_ROPE // 2
V_DIM = 64
Q_LORA = 384
KV_LORA = 256
POOL_WINDOWS = (2, 4, 8, 16)
POOL_WIDTH = 512
POOL_GC = POOL_WIDTH // len(POOL_WINDOWS)
D_FF = 4 * D_MODEL
ROPE_THETA = 10000.0
NORM_EPS = 1e-6
ATTN_SCALE = (QK_NOPE + QK_ROPE) ** -0.5

HEAD_SLAB = 128
POOL_HALO = max(POOL_WINDOWS) // 2
VMEM_LIMIT_BYTES = 56 * 1024 * 1024

BF16 = jnp.bfloat16
F32 = jnp.float32


def _dot(a, b):
    return jnp.dot(a, b, preferred_element_type=F32)


def _dot_nt(a, b):
    return lax.dot_general(a, b, (((1,), (1,)), ((), ())), preferred_element_type=F32)


def _rms(x, g):
    ms = jnp.mean(x * x, axis=-1, keepdims=True)
    return x * lax.rsqrt(ms + NORM_EPS) * g


def _modulated(x, g, shift, scale):
    return _rms(x, g) * (1.0 + scale) + shift


def _adaln_kernel(cond_ref, w_ref, b_ref, o_ref):
    c = cond_ref[...]
    act = (c * jax.nn.sigmoid(c)).astype(BF16)
    o_ref[...] = _dot(act, w_ref[...].astype(BF16)) + b_ref[...]


def _adaln(cond, w_ada, b_ada):
    rows, d = cond.shape
    n = w_ada.shape[1]
    tn = 1536
    return pl.pallas_call(
        _adaln_kernel,
        out_shape=jax.ShapeDtypeStruct((rows, n), F32),
        grid=(n // tn,),
        in_specs=[pl.BlockSpec((rows, d), lambda j: (0, 0)),
                  pl.BlockSpec((d, tn), lambda j: (0, j)),
                  pl.BlockSpec((1, tn), lambda j: (0, j))],
        out_specs=pl.BlockSpec((rows, tn), lambda j: (0, j)),
        compiler_params=pltpu.CompilerParams(
            dimension_semantics=("arbitrary",), vmem_limit_bytes=VMEM_LIMIT_BYTES),
        name="adaln",
    )(cond, w_ada, b_ada)


_A_CQ = (0, Q_LORA)
_A_CKV = (Q_LORA, Q_LORA + KV_LORA)
_A_KR = (_A_CKV[1], _A_CKV[1] + HEAD_SLAB)
_A_KRS = (_A_KR[1], _A_KR[1] + HEAD_SLAB)
_A_POOL = (_A_KRS[1], _A_KRS[1] + POOL_WIDTH)
_A_WIDTH = _A_POOL[1]


def _inproj_latent_kernel(x_ref, mod_ref, g1_ref, wa_ref, qg_ref, kvg_ref, wq_ref, wk_ref, wvt_ref,
                          cos_ref, sin_ref, q_ref, k_ref, vt_ref, pool_ref):
    x = x_ref[0]
    mod = mod_ref[0]
    h = _modulated(x, g1_ref[...], mod[0:1], mod[1:2]).astype(BF16)
    a = _dot(h, wa_ref[...])
    pool_ref[0] = a[:, _A_POOL[0]:_A_POOL[1]]
    cos = cos_ref[...]
    sin = sin_ref[...]

    cqn = _rms(a[:, _A_CQ[0]:_A_CQ[1]], qg_ref[...]).astype(BF16)
    qq = _dot(cqn, wq_ref[...])
    hw = N_HEADS * HEAD_SLAB
    for hd in range(N_HEADS):
        lo = hd * HEAD_SLAB
        qh = qq[:, lo:lo + HEAD_SLAB] * cos + qq[:, hw + lo:hw + lo + HEAD_SLAB] * sin
        q_ref[0, hd] = qh.astype(BF16)

    ckvn = _rms(a[:, _A_CKV[0]:_A_CKV[1]], kvg_ref[...]).astype(BF16)
    kk = _dot(ckvn, wk_ref[...])
    kr = a[:, _A_KR[0]:_A_KR[1]] * cos + a[:, _A_KRS[0]:_A_KRS[1]] * sin
    for hd in range(N_HEADS):
        lo = hd * HEAD_SLAB
        k_ref[0, hd] = (kk[:, lo:lo + HEAD_SLAB] + kr).astype(BF16)
    vt_ref[0] = _dot_nt(wvt_ref[...], ckvn).astype(BF16)


def _inproj_ctx_kernel(x_ref, mod_ref, g1_ref, wa_ref, kvg_ref, wk_ref, wvt_ref, k_ref, vt_ref):
    x = x_ref[0]
    mod = mod_ref[0]
    h = _modulated(x, g1_ref[...], mod[0:1], mod[1:2]).astype(BF16)
    a = _dot(h, wa_ref[:, _A_CKV[0]:_A_KR[1]])
    ckvn = _rms(a[:, :KV_LORA], kvg_ref[...]).astype(BF16)
    kk = _dot(ckvn, wk_ref[...])
    kr = a[:, KV_LORA:KV_LORA + HEAD_SLAB]
    for hd in range(N_HEADS):
        lo = hd * HEAD_SLAB
        k_ref[0, hd] = (kk[:, lo:lo + HEAD_SLAB] + kr).astype(BF16)
    vt_ref[0] = _dot_nt(wvt_ref[...], ckvn).astype(BF16)


def _const_spec(shape):
    nd = len(shape)
    return pl.BlockSpec(shape, lambda *_: (0,) * nd)


def _inproj_ctx(ctx, mod, ctx_row, g1, wa, kvg, wk, wvt):
    b, lc, d = ctx.shape
    return pl.pallas_call(
        _inproj_ctx_kernel,
        out_shape=(jax.ShapeDtypeStruct((b, N_HEADS, lc, HEAD_SLAB), BF16),
                   jax.ShapeDtypeStruct((b, N_HEADS * V_DIM, lc), BF16)),
        grid=(b,),
        in_specs=[pl.BlockSpec((1, lc, d), lambda bi: (bi, 0, 0)),
                  pl.BlockSpec((1, 6, d), lambda bi: (ctx_row, 0, 0)),
                  _const_spec(g1.shape), _const_spec(wa.shape), _const_spec(kvg.shape),
                  _const_spec(wk.shape), _const_spec(wvt.shape)],
        out_specs=(pl.BlockSpec((1, N_HEADS, lc, HEAD_SLAB), lambda bi: (bi, 0, 0, 0)),
                   pl.BlockSpec((1, N_HEADS * V_DIM, lc), lambda bi: (bi, 0, 0))),
        compiler_params=pltpu.CompilerParams(
            dimension_semantics=("arbitrary",), vmem_limit_bytes=VMEM_LIMIT_BYTES),
        name="inproj_ctx",
    )(ctx, mod, g1, wa, kvg, wk, wvt)


def _inproj_latent(x, mod, g1, wa, qg, kvg, wq, wk, wvt, cos, sin, tm):
    b, l, d = x.shape
    nt = l // tm
    return pl.pallas_call(
        _inproj_latent_kernel,
        out_shape=(jax.ShapeDtypeStruct((b, N_HEADS, l, HEAD_SLAB), BF16),
                   jax.ShapeDtypeStruct((b, N_HEADS, l, HEAD_SLAB), BF16),
                   jax.ShapeDtypeStruct((b, N_HEADS * V_DIM, l), BF16),
                   jax.ShapeDtypeStruct((b, l, POOL_WIDTH), F32)),
        grid=(b, nt),
        in_specs=[pl.BlockSpec((1, tm, d), lambda bi, i: (bi, i, 0)),
                  pl.BlockSpec((1, 6, d), lambda bi, i: (bi, 0, 0)),
                  _const_spec(g1.shape), _const_spec(wa.shape), _const_spec(qg.shape),
                  _const_spec(kvg.shape), _const_spec(wq.shape), _const_spec(wk.shape),
                  _const_spec(wvt.shape),
                  pl.BlockSpec((tm, HEAD_SLAB), lambda bi, i: (i, 0)),
                  pl.BlockSpec((tm, HEAD_SLAB), lambda bi, i: (i, 0))],
        out_specs=(pl.BlockSpec((1, N_HEADS, tm, HEAD_SLAB), lambda bi, i: (bi, 0, i, 0)),
                   pl.BlockSpec((1, N_HEADS, tm, HEAD_SLAB), lambda bi, i: (bi, 0, i, 0)),
                   pl.BlockSpec((1, N_HEADS * V_DIM, tm), lambda bi, i: (bi, 0, i)),
                   pl.BlockSpec((1, tm, POOL_WIDTH), lambda bi, i: (bi, i, 0))),
        compiler_params=pltpu.CompilerParams(
            dimension_semantics=("arbitrary", "arbitrary"), vmem_limit_bytes=VMEM_LIMIT_BYTES),
        name="inproj_latent",
    )(x, mod, g1, wa, qg, kvg, wq, wk, wvt, cos, sin)


HEADS_PER_STEP = 2
EXP2_SCALE = ATTN_SCALE * math.log2(math.e)
ATTN_KEY_CHUNK = 768
S_RING = 2
P_RING = 2


def _key_chunks(n_latent, n_ctx):
    kc = ATTN_KEY_CHUNK
    chunks = [[(False, s, min(kc, n_latent - s))] for s in range(0, n_latent, kc)]
    if chunks[-1][0][2] + n_ctx <= kc:
        chunks[-1].append((True, 0, n_ctx))
    else:
        chunks.append([(True, 0, n_ctx)])
    return chunks


def _attn_kernel(q_ref, k_ref, vt_ref, kc_ref, vct_ref, o_ref, *scratch):
    s_refs = scratch[:S_RING]
    p_refs = scratch[S_RING:S_RING + P_RING]
    acc_ref, out_ref = scratch[S_RING + P_RING:]
    tq = q_ref.shape[2]
    chunks = _key_chunks(k_ref.shape[2], kc_ref.shape[2])
    n_chunks = len(chunks)

    def one_head(hh, _):
        vrow0 = pl.multiple_of(hh * V_DIM, V_DIM)

        def scores(s_ref, chunk):
            cmax, row = None, 0
            for is_ctx, start, size in chunk:
                keys = (kc_ref if is_ctx else k_ref)[0, hh, start:start + size, :]
                s = _dot_nt(keys, q_ref[0, hh])
                s_ref[row:row + size, :] = s
                part = jnp.max(s, axis=0, keepdims=True)
                cmax = part if cmax is None else jnp.maximum(cmax, part)
                row += size
            return cmax

        def values(p_ref, chunk):
            pv, row = None, 0
            for is_ctx, start, size in chunk:
                vt_blk = (vct_ref if is_ctx else vt_ref)[0, pl.ds(vrow0, V_DIM), start:start + size]
                part = _dot(vt_blk, p_ref[row:row + size, :])
                pv = part if pv is None else pv + part
                row += size
            return pv

        m = jnp.full((1, tq), -jnp.inf, F32)
        l = jnp.zeros((1, tq), F32)
        cmax = scores(s_refs[0], chunks[0])
        for c, chunk in enumerate(chunks):
            if c + 1 < n_chunks:
                cmax_nxt = scores(s_refs[(c + 1) % S_RING], chunks[c + 1])
            if c > 0:
                pv_prev = values(p_refs[(c - 1) % P_RING], chunks[c - 1])
            rows = sum(size for _, _, size in chunk)
            m_new = jnp.maximum(m, cmax)
            alpha = jnp.exp2((m - m_new) * EXP2_SCALE)
            p = jnp.exp2((s_refs[c % S_RING][0:rows, :] - m_new) * EXP2_SCALE)
            l = alpha * l + jnp.sum(p, axis=0, keepdims=True)
            p_refs[c % P_RING][0:rows, :] = p.astype(BF16)
            if c == 1:
                acc_ref[...] = pv_prev * alpha
            elif c > 1:
                acc_ref[...] = (acc_ref[...] + pv_prev) * alpha
            m = m_new
            if c + 1 < n_chunks:
                cmax = cmax_nxt
        last = n_chunks - 1
        out_ref[hh] = (acc_ref[...] + values(p_refs[last % P_RING], chunks[last])) / l
        return 0

    lax.fori_loop(0, HEADS_PER_STEP, one_head, 0)
    o = out_ref[...].reshape(HEADS_PER_STEP * V_DIM, tq)
    o_ref[0] = o.T.astype(BF16)


def _attention(q, k, vt, kc, vct, tq):
    b, nh, l, _ = q.shape
    lc = kc.shape[2]
    assert len(_key_chunks(l, lc)) >= 2
    hp = HEADS_PER_STEP
    lanes = hp * V_DIM
    return pl.pallas_call(
        _attn_kernel,
        out_shape=jax.ShapeDtypeStruct((b, l, nh * V_DIM), BF16),
        grid=(b, nh // hp, l // tq),
        in_specs=[pl.BlockSpec((1, hp, tq, HEAD_SLAB), lambda bi, g, i: (bi, g, i, 0)),
                  pl.BlockSpec((1, hp, l, HEAD_SLAB), lambda bi, g, i: (bi, g, 0, 0)),
                  pl.BlockSpec((1, lanes, l), lambda bi, g, i: (bi, g, 0)),
                  pl.BlockSpec((1, hp, lc, HEAD_SLAB), lambda bi, g, i: (bi, g, 0, 0)),
                  pl.BlockSpec((1, lanes, lc), lambda bi, g, i: (bi, g, 0))],
        out_specs=pl.BlockSpec((1, tq, lanes), lambda bi, g, i: (bi, i, g)),
        scratch_shapes=([pltpu.VMEM((ATTN_KEY_CHUNK, tq), F32)] * S_RING
                        + [pltpu.VMEM((ATTN_KEY_CHUNK, tq), BF16)] * P_RING
                        + [pltpu.VMEM((V_DIM, tq), F32), pltpu.VMEM((hp, V_DIM, tq), F32)]),
        compiler_params=pltpu.CompilerParams(
            dimension_semantics=("arbitrary", "arbitrary", "arbitrary"),
            vmem_limit_bytes=VMEM_LIMIT_BYTES),
        name="attention",
    )(q, k, vt, kc, vct)


def _merge_mlp_kernel(x_ref, mod_ref, g1_ref, attn_ref, pool_ref, prev_ref, next_ref, wg_ref, wbm_ref,
                      pw_ref, ps_ref, wbp_ref, wo_ref, g2_ref, w1_ref, w2_ref, gf_ref, o_ref, ext_ref,
                      *, seq_len):
    tm = x_ref.shape[1]
    i = pl.program_id(1)
    n_tiles = pl.num_programs(1)
    x = x_ref[0]
    mod = mod_ref[0]
    h = _modulated(x, g1_ref[...], mod[0:1], mod[1:2]).astype(BF16)
    gates = jax.nn.sigmoid(_dot(h, wg_ref[...]))

    halo = POOL_HALO
    ext_ref[0:halo, :] = jnp.where(i > 0, prev_ref[0], 0.0)
    ext_ref[halo:halo + tm, :] = pool_ref[0]
    ext_ref[halo + tm:halo + tm + halo, :] = jnp.where(i < n_tiles - 1, next_ref[0], 0.0)

    t = i * tm + lax.broadcasted_iota(jnp.int32, (tm, POOL_GC), 0)
    pooled = []
    for gi, w in enumerate(POOL_WINDOWS):
        cols = slice(gi * POOL_GC, (gi + 1) * POOL_GC)
        acc = ext_ref[halo - w // 2:halo - w // 2 + tm, cols]
        for dlt in range(1, w):
            off = halo - w // 2 + dlt
            acc = acc + ext_ref[off:off + tm, cols]
        cnt = (jnp.clip(t + w // 2, 0, seq_len) - jnp.clip(t - w // 2, 0, seq_len)).astype(F32)
        dg = (acc / cnt - ext_ref[halo:halo + tm, cols]).astype(BF16)
        pooled.append(_dot(dg, pw_ref[gi]))
    pooled = (jnp.concatenate(pooled, axis=1) * ps_ref[...]).astype(BF16)

    d = D_MODEL
    merged = (gates[:, :d] * _dot(attn_ref[0], wbm_ref[...])
              + gates[:, d:] * _dot(pooled, wbp_ref[...]))
    x1 = x + mod[2:3] * _dot(merged.astype(BF16), wo_ref[...])

    h2 = _modulated(x1, g2_ref[...], mod[3:4], mod[4:5]).astype(BF16)
    acc = jnp.zeros(x.shape, F32)
    for c in range(D_FF // FF_CHUNK):
        cols = slice(c * FF_CHUNK, (c + 1) * FF_CHUNK)
        hid = jnp.maximum(_dot(h2, w1_ref[:, cols]), 0.0)
        acc = acc + _dot((hid * hid).astype(BF16), w2_ref[cols, :])
    x2 = x1 + mod[5:6] * acc
    o_ref[0] = _rms(x2, gf_ref[...])


FF_CHUNK = 1024


def _resident_spec(shape):
    nd = len(shape)
    return pl.BlockSpec(shape, lambda *_: (0,) * nd, pipeline_mode=pl.Buffered(1))


def _merge_mlp(x, mod, g1, attn, pool_in, wg, wbm, pw, ps, wbp, wo, g2, w1, w2, gf, tm):
    b, l, d = x.shape
    nt = l // tm
    hb = tm // POOL_HALO
    last_hb = l // POOL_HALO - 1
    kern = functools.partial(_merge_mlp_kernel, seq_len=l)
    weights = (wg, wbm, pw, ps, wbp, wo, g2, w1, w2, gf)
    return pl.pallas_call(
        kern,
        out_shape=jax.ShapeDtypeStruct((b, l, d), F32),
        grid=(b, nt),
        in_specs=[pl.BlockSpec((1, tm, d), lambda bi, i: (bi, i, 0)),
                  pl.BlockSpec((1, 6, d), lambda bi, i: (bi, 0, 0)),
                  _resident_spec(g1.shape),
                  pl.BlockSpec((1, tm, attn.shape[2]), lambda bi, i: (bi, i, 0)),
                  pl.BlockSpec((1, tm, POOL_WIDTH), lambda bi, i: (bi, i, 0)),
                  pl.BlockSpec((1, POOL_HALO, POOL_WIDTH),
                               lambda bi, i: (bi, jnp.maximum(i * hb - 1, 0), 0)),
                  pl.BlockSpec((1, POOL_HALO, POOL_WIDTH),
                               lambda bi, i: (bi, jnp.minimum((i + 1) * hb, last_hb), 0))]
                 + [_resident_spec(w.shape) for w in weights],
        out_specs=pl.BlockSpec((1, tm, d), lambda bi, i: (bi, i, 0)),
        scratch_shapes=[pltpu.VMEM((tm + 2 * POOL_HALO, POOL_WIDTH), F32)],
        compiler_params=pltpu.CompilerParams(
            dimension_semantics=("arbitrary", "arbitrary"), vmem_limit_bytes=VMEM_LIMIT_BYTES),
        name="merge_mlp",
    )(x, mod, g1, attn, pool_in, pool_in, pool_in, *weights)


def _rope_swap_perm():
    half = ROPE_AXIS // 2
    idx = []
    for base in (0, ROPE_AXIS):
        idx += list(range(base + half, base + ROPE_AXIS)) + list(range(base, base + half))
    return np.asarray(idx)


def _rope_tables(seq_len):
    t = np.arange(seq_len)
    row = (t // GRID_W).astype(np.float64)
    col = (t % GRID_W).astype(np.float64)
    inv_freq = ROPE_THETA ** (-np.arange(0, ROPE_AXIS, 2, dtype=np.float64) / ROPE_AXIS)
    ang_row = row[:, None] * inv_freq
    ang_col = col[:, None] * inv_freq
    ones = np.ones((seq_len, QK_NOPE), np.float64)
    pad = np.zeros((seq_len, HEAD_SLAB - QK_NOPE - QK_ROPE), np.float64)
    cos_t = np.concatenate([ones, np.cos(ang_row), np.cos(ang_row),
                            np.cos(ang_col), np.cos(ang_col), pad + 1.0], axis=1)
    sin_t = np.concatenate([0.0 * ones, -np.sin(ang_row), np.sin(ang_row),
                            -np.sin(ang_col), np.sin(ang_col), pad], axis=1)
    return jnp.asarray(cos_t, F32), jnp.asarray(sin_t, F32)


def _pack_weights(w_in, w_uq, w_ukv):
    perm = _rope_swap_perm()
    d = w_in.shape[0]
    s0, s1, s2, s3, s4 = (Q_LORA, Q_LORA + KV_LORA, Q_LORA + KV_LORA + QK_ROPE,
                          Q_LORA + KV_LORA + QK_ROPE + POOL_WIDTH,
                          Q_LORA + KV_LORA + QK_ROPE + POOL_WIDTH + D_MODEL)
    w_kr = w_in[:, s1:s2]
    zl = jnp.zeros((d, QK_NOPE), w_in.dtype)
    zr = jnp.zeros((d, HEAD_SLAB - QK_NOPE - QK_ROPE), w_in.dtype)
    wa = jnp.concatenate([w_in[:, :s1], zl, w_kr, zr, zl, w_kr[:, perm], zr, w_in[:, s2:s3]], axis=1)
    wg = w_in[:, s3:]

    dq = QK_NOPE + QK_ROPE
    wq3 = w_uq.reshape(Q_LORA, N_HEADS, dq)
    zq = jnp.zeros((Q_LORA, N_HEADS, HEAD_SLAB - dq), w_uq.dtype)
    wq_plain = jnp.concatenate([wq3, zq], axis=2).reshape(Q_LORA, N_HEADS * HEAD_SLAB)
    wq_swap = jnp.concatenate([jnp.zeros((Q_LORA, N_HEADS, QK_NOPE), w_uq.dtype),
                               wq3[:, :, QK_NOPE:][:, :, perm], zq], axis=2
                              ).reshape(Q_LORA, N_HEADS * HEAD_SLAB)
    wq = jnp.concatenate([wq_plain, wq_swap], axis=1)

    wkv3 = w_ukv.reshape(KV_LORA, N_HEADS, QK_NOPE + V_DIM)
    wk = jnp.concatenate([wkv3[:, :, :QK_NOPE],
                          jnp.zeros((KV_LORA, N_HEADS, HEAD_SLAB - QK_NOPE), w_ukv.dtype)],
                         axis=2).reshape(KV_LORA, N_HEADS * HEAD_SLAB)
    wvt = wkv3[:, :, QK_NOPE:].reshape(KV_LORA, N_HEADS * V_DIM).T
    return wa.astype(BF16), wg.astype(BF16), wq.astype(BF16), wk.astype(BF16), wvt.astype(BF16)


TOKEN_TILE = 512
Q_TILE = 512


def kernel(x, c, ctx, c_ctx, w_ada, b_ada, norm1_g, w_in, q_norm_g, kv_norm_g, w_uq, w_ukv,
           w_br_mla, pool_w, pool_scale, w_br_pool, w_out, norm2_g, w_mlp1, w_mlp2, final_g):
    b, l, d = x.shape
    assert w_ada.shape[0] == 1, "single-layer block"
    assert l % TOKEN_TILE == 0 and l % Q_TILE == 0 and l % GRID_W == 0

    cond_rows = 16
    cond = jnp.concatenate([c, c_ctx[None, :], jnp.zeros((cond_rows - b - 1, d), F32)], axis=0)
    mod = _adaln(cond, w_ada[0], b_ada[0][None, :]).reshape(cond_rows, 6, d)

    wa, wg, wq, wk, wvt = _pack_weights(w_in[0], w_uq[0], w_ukv[0])
    cos_t, sin_t = _rope_tables(l)
    g1 = norm1_g[0][None, :]
    qg = q_norm_g[0][None, :]
    kvg = kv_norm_g[0][None, :]

    kc, vct = _inproj_ctx(ctx, mod, b, g1, wa, kvg, wk, wvt)
    q, k, vt, pool_in = _inproj_latent(x, mod, g1, wa, qg, kvg, wq, wk, wvt, cos_t, sin_t,
                                       TOKEN_TILE)
    attn = _attention(q, k, vt, kc, vct, Q_TILE)
    return _merge_mlp(x, mod, g1, attn, pool_in, wg, w_br_mla[0].astype(BF16),
                      pool_w[0].astype(BF16), pool_scale[0][None, :], w_br_pool[0].astype(BF16),
                      w_out[0].astype(BF16), norm2_g[0][None, :], w_mlp1[0].astype(BF16),
                      w_mlp2[0].astype(BF16), final_g[None, :], TOKEN_TILE)
```

```python
import functools
import math

import jax
import jax.numpy as jnp
import numpy as np
from jax import lax
from jax.experimental import pallas as pl
from jax.experimental.pallas import tpu as pltpu

D_MODEL = 1024
GRID_W = 64
N_HEADS = 8
QK_NOPE = 64
QK_ROPE = 32
ROPE_AXIS = QK_ROPE // 2
V_DIM = 64
Q_LORA = 384
KV_LORA = 256
POOL_WINDOWS = (2, 4, 8, 16)
POOL_WIDTH = 512
POOL_GC = POOL_WIDTH // len(POOL_WINDOWS)
D_FF = 4 * D_MODEL
ROPE_THETA = 10000.0
NORM_EPS = 1e-6
ATTN_SCALE = (QK_NOPE + QK_ROPE) ** -0.5

HEAD_SLAB = 128
EXP2_SCALE = ATTN_SCALE * math.log2(math.e)
POOL_HALO = max(POOL_WINDOWS) // 2
VMEM_LIMIT_BYTES = 56 * 1024 * 1024

BF16 = jnp.bfloat16
F32 = jnp.float32


def _dot(a, b):
    return jnp.dot(a, b, preferred_element_type=F32)


def _dot_nt(a, b):
    return lax.dot_general(a, b, (((1,), (1,)), ((), ())), preferred_element_type=F32)


def _rms(x, g):
    ms = jnp.mean(x * x, axis=-1, keepdims=True)
    return x * lax.rsqrt(ms + NORM_EPS) * g


def _modulated(x, g, shift, scale):
    return _rms(x, g) * (1.0 + scale) + shift


def _adaln_kernel(cond_ref, w_ref, b_ref, o_ref):
    c = cond_ref[...]
    act = (c * jax.nn.sigmoid(c)).astype(BF16)
    o_ref[...] = _dot(act, w_ref[...].astype(BF16)) + b_ref[...]


def _adaln(cond, w_ada, b_ada):
    rows, d = cond.shape
    n = w_ada.shape[1]
    tn = 1536
    return pl.pallas_call(
        _adaln_kernel,
        out_shape=jax.ShapeDtypeStruct((rows, n), F32),
        grid=(n // tn,),
        in_specs=[pl.BlockSpec((rows, d), lambda j: (0, 0)),
                  pl.BlockSpec((d, tn), lambda j: (0, j)),
                  pl.BlockSpec((1, tn), lambda j: (0, j))],
        out_specs=pl.BlockSpec((rows, tn), lambda j: (0, j)),
        compiler_params=pltpu.CompilerParams(
            dimension_semantics=("arbitrary",), vmem_limit_bytes=VMEM_LIMIT_BYTES),
        name="adaln",
    )(cond, w_ada, b_ada)


_A_CQ = (0, Q_LORA)
_A_CKV = (Q_LORA, Q_LORA + KV_LORA)
_A_KR = (_A_CKV[1], _A_CKV[1] + HEAD_SLAB)
_A_KRS = (_A_KR[1], _A_KR[1] + HEAD_SLAB)
_A_POOL = (_A_KRS[1], _A_KRS[1] + POOL_WIDTH)
_A_WIDTH = _A_POOL[1]


def _inproj_latent_kernel(x_ref, mod_ref, g1_ref, wa_ref, qg_ref, kvg_ref, wq_ref, wk_ref, wvt_ref,
                          cos_ref, sin_ref, q_ref, k_ref, vt_ref, pool_ref):
    x = x_ref[0]
    mod = mod_ref[0]
    h = _modulated(x, g1_ref[...], mod[0:1], mod[1:2]).astype(BF16)
    a = _dot(h, wa_ref[...])
    pool_ref[0] = a[:, _A_POOL[0]:_A_POOL[1]]
    cos = cos_ref[...]
    sin = sin_ref[...]

    cqn = _rms(a[:, _A_CQ[0]:_A_CQ[1]], qg_ref[...]).astype(BF16)
    qq = _dot(cqn, wq_ref[...])
    hw = N_HEADS * HEAD_SLAB
    for hd in range(N_HEADS):
        lo = hd * HEAD_SLAB
        qh = qq[:, lo:lo + HEAD_SLAB] * cos + qq[:, hw + lo:hw + lo + HEAD_SLAB] * sin
        q_ref[0, hd] = (qh * EXP2_SCALE).astype(BF16)

    ckvn = _rms(a[:, _A_CKV[0]:_A_CKV[1]], kvg_ref[...]).astype(BF16)
    kk = _dot(ckvn, wk_ref[...])
    kr = a[:, _A_KR[0]:_A_KR[1]] * cos + a[:, _A_KRS[0]:_A_KRS[1]] * sin
    for hd in range(N_HEADS):
        lo = hd * HEAD_SLAB
        k_ref[0, hd] = (kk[:, lo:lo + HEAD_SLAB] + kr).astype(BF16)
    vt_ref[0] = _dot_nt(wvt_ref[...], ckvn).astype(BF16)


def _inproj_ctx_kernel(x_ref, mod_ref, g1_ref, wa_ref, kvg_ref, wk_ref, wvt_ref, k_ref, vt_ref):
    x = x_ref[0]
    mod = mod_ref[0]
    h = _modulated(x, g1_ref[...], mod[0:1], mod[1:2]).astype(BF16)
    a = _dot(h, wa_ref[:, _A_CKV[0]:_A_KR[1]])
    ckvn = _rms(a[:, :KV_LORA], kvg_ref[...]).astype(BF16)
    kk = _dot(ckvn, wk_ref[...])
    kr = a[:, KV_LORA:KV_LORA + HEAD_SLAB]
    for hd in range(N_HEADS):
        lo = hd * HEAD_SLAB
        k_ref[0, hd] = (kk[:, lo:lo + HEAD_SLAB] + kr).astype(BF16)
    vt_ref[0] = _dot_nt(wvt_ref[...], ckvn).astype(BF16)


def _const_spec(shape):
    nd = len(shape)
    return pl.BlockSpec(shape, lambda *_: (0,) * nd)


def _inproj_ctx(ctx, mod, ctx_row, g1, wa, kvg, wk, wvt):
    b, lc, d = ctx.shape
    return pl.pallas_call(
        _inproj_ctx_kernel,
        out_shape=(jax.ShapeDtypeStruct((b, N_HEADS, lc, HEAD_SLAB), BF16),
                   jax.ShapeDtypeStruct((b, N_HEADS * V_DIM, lc), BF16)),
        grid=(b,),
        in_specs=[pl.BlockSpec((1, lc, d), lambda bi: (bi, 0, 0)),
                  pl.BlockSpec((1, 6, d), lambda bi: (ctx_row, 0, 0)),
                  _const_spec(g1.shape), _const_spec(wa.shape), _const_spec(kvg.shape),
                  _const_spec(wk.shape), _const_spec(wvt.shape)],
        out_specs=(pl.BlockSpec((1, N_HEADS, lc, HEAD_SLAB), lambda bi: (bi, 0, 0, 0)),
                   pl.BlockSpec((1, N_HEADS * V_DIM, lc), lambda bi: (bi, 0, 0))),
        compiler_params=pltpu.CompilerParams(
            dimension_semantics=("arbitrary",), vmem_limit_bytes=VMEM_LIMIT_BYTES),
        name="inproj_ctx",
    )(ctx, mod, g1, wa, kvg, wk, wvt)


def _inproj_latent(x, mod, g1, wa, qg, kvg, wq, wk, wvt, cos, sin, tm):
    b, l, d = x.shape
    nt = l // tm
    return pl.pallas_call(
        _inproj_latent_kernel,
        out_shape=(jax.ShapeDtypeStruct((b, N_HEADS, l, HEAD_SLAB), BF16),
                   jax.ShapeDtypeStruct((b, N_HEADS, l, HEAD_SLAB), BF16),
                   jax.ShapeDtypeStruct((b, N_HEADS * V_DIM, l), BF16),
                   jax.ShapeDtypeStruct((b, l, POOL_WIDTH), F32)),
        grid=(b, nt),
        in_specs=[pl.BlockSpec((1, tm, d), lambda bi, i: (bi, i, 0)),
                  pl.BlockSpec((1, 6, d), lambda bi, i: (bi, 0, 0)),
                  _const_spec(g1.shape), _const_spec(wa.shape), _const_spec(qg.shape),
                  _const_spec(kvg.shape), _const_spec(wq.shape), _const_spec(wk.shape),
                  _const_spec(wvt.shape),
                  pl.BlockSpec((tm, HEAD_SLAB), lambda bi, i: (i, 0)),
                  pl.BlockSpec((tm, HEAD_SLAB), lambda bi, i: (i, 0))],
        out_specs=(pl.BlockSpec((1, N_HEADS, tm, HEAD_SLAB), lambda bi, i: (bi, 0, i, 0)),
                   pl.BlockSpec((1, N_HEADS, tm, HEAD_SLAB), lambda bi, i: (bi, 0, i, 0)),
                   pl.BlockSpec((1, N_HEADS * V_DIM, tm), lambda bi, i: (bi, 0, i)),
                   pl.BlockSpec((1, tm, POOL_WIDTH), lambda bi, i: (bi, i, 0))),
        compiler_params=pltpu.CompilerParams(
            dimension_semantics=("arbitrary", "arbitrary"), vmem_limit_bytes=VMEM_LIMIT_BYTES),
        name="inproj_latent",
    )(x, mod, g1, wa, qg, kvg, wq, wk, wvt, cos, sin)


HEADS_PER_STEP = 2
ATTN_KEY_CHUNK = 768
S_RING = 2
P_RING = 2


def _key_chunks(n_latent, n_ctx):
    kc = ATTN_KEY_CHUNK
    chunks = [[(False, s, min(kc, n_latent - s))] for s in range(0, n_latent, kc)]
    if chunks[-1][0][2] + n_ctx <= kc:
        chunks[-1].append((True, 0, n_ctx))
    else:
        chunks.append([(True, 0, n_ctx)])
    return chunks


def _attn_kernel(q_ref, k_ref, vt_ref, kc_ref, vct_ref, o_ref, *scratch):
    s_refs = scratch[:S_RING]
    p_refs = scratch[S_RING:S_RING + P_RING]
    acc_ref, out_ref = scratch[S_RING + P_RING:]
    tq = q_ref.shape[2]
    chunks = _key_chunks(k_ref.shape[2], kc_ref.shape[2])
    n_chunks = len(chunks)

    def one_head(hh, _):
        vrow0 = pl.multiple_of(hh * V_DIM, V_DIM)

        def scores(s_ref, chunk):
            cmax, row = None, 0
            for is_ctx, start, size in chunk:
                keys = (kc_ref if is_ctx else k_ref)[0, hh, start:start + size, :]
                s = _dot_nt(keys, q_ref[0, hh])
                s_ref[row:row + size, :] = s
                part = jnp.max(s, axis=0, keepdims=True)
                cmax = part if cmax is None else jnp.maximum(cmax, part)
                row += size
            return cmax

        def values(p_ref, chunk):
            pv, row = None, 0
            for is_ctx, start, size in chunk:
                vt_blk = (vct_ref if is_ctx else vt_ref)[0, pl.ds(vrow0, V_DIM), start:start + size]
                part = _dot(vt_blk, p_ref[row:row + size, :])
                pv = part if pv is None else pv + part
                row += size
            return pv

        m = jnp.full((1, tq), -jnp.inf, F32)
        l = jnp.zeros((1, tq), F32)
        cmax = scores(s_refs[0], chunks[0])
        for c, chunk in enumerate(chunks):
            if c + 1 < n_chunks:
                cmax_nxt = scores(s_refs[(c + 1) % S_RING], chunks[c + 1])
            if c > 0:
                pv_prev = values(p_refs[(c - 1) % P_RING], chunks[c - 1])
            rows = sum(size for _, _, size in chunk)
            m_new = jnp.maximum(m, cmax)
            alpha = jnp.exp2(m - m_new)
            p = jnp.exp2(s_refs[c % S_RING][0:rows, :] - m_new)
            l = alpha * l + jnp.sum(p, axis=0, keepdims=True)
            p_refs[c % P_RING][0:rows, :] = p.astype(BF16)
            if c == 1:
                acc_ref[...] = pv_prev * alpha
            elif c > 1:
                acc_ref[...] = (acc_ref[...] + pv_prev) * alpha
            m = m_new
            if c + 1 < n_chunks:
                cmax = cmax_nxt
        last = n_chunks - 1
        out_ref[hh] = (acc_ref[...] + values(p_refs[last % P_RING], chunks[last])) / l
        return 0

    lax.fori_loop(0, HEADS_PER_STEP, one_head, 0)
    o = out_ref[...].reshape(HEADS_PER_STEP * V_DIM, tq)
    o_ref[0] = o.T.astype(BF16)


def _attention(q, k, vt, kc, vct, tq):
    b, nh, l, _ = q.shape
    lc = kc.shape[2]
    assert len(_key_chunks(l, lc)) >= 2
    hp = HEADS_PER_STEP
    lanes = hp * V_DIM
    return pl.pallas_call(
        _attn_kernel,
        out_shape=jax.ShapeDtypeStruct((b, l, nh * V_DIM), BF16),
        grid=(b, nh // hp, l // tq),
        in_specs=[pl.BlockSpec((1, hp, tq, HEAD_SLAB), lambda bi, g, i: (bi, g, i, 0)),
                  pl.BlockSpec((1, hp, l, HEAD_SLAB), lambda bi, g, i: (bi, g, 0, 0)),
                  pl.BlockSpec((1, lanes, l), lambda bi, g, i: (bi, g, 0)),
                  pl.BlockSpec((1, hp, lc, HEAD_SLAB), lambda bi, g, i: (bi, g, 0, 0)),
                  pl.BlockSpec((1, lanes, lc), lambda bi, g, i: (bi, g, 0))],
        out_specs=pl.BlockSpec((1, tq, lanes), lambda bi, g, i: (bi, i, g)),
        scratch_shapes=([pltpu.VMEM((ATTN_KEY_CHUNK, tq), F32)] * S_RING
                        + [pltpu.VMEM((ATTN_KEY_CHUNK, tq), BF16)] * P_RING
                        + [pltpu.VMEM((V_DIM, tq), F32), pltpu.VMEM((hp, V_DIM, tq), F32)]),
        compiler_params=pltpu.CompilerParams(
            dimension_semantics=("arbitrary", "arbitrary", "arbitrary"),
            vmem_limit_bytes=VMEM_LIMIT_BYTES),
        name="attention",
    )(q, k, vt, kc, vct)


def _merge_mlp_kernel(x_ref, mod_ref, g1_ref, attn_ref, pool_ref, prev_ref, next_ref, wg_ref, wbm_ref,
                      pw_ref, ps_ref, wbp_ref, wo_ref, g2_ref, w1_ref, w2_ref, gf_ref, o_ref, ext_ref,
                      *, seq_len):
    tm = x_ref.shape[1]
    i = pl.program_id(1)
    n_tiles = pl.num_programs(1)
    x = x_ref[0]
    mod = mod_ref[0]
    h = _modulated(x, g1_ref[...], mod[0:1], mod[1:2]).astype(BF16)
    gates = jax.nn.sigmoid(_dot(h, wg_ref[...]))

    halo = POOL_HALO
    ext_ref[0:halo, :] = jnp.where(i > 0, prev_ref[0], 0.0)
    ext_ref[halo:halo + tm, :] = pool_ref[0]
    ext_ref[halo + tm:halo + tm + halo, :] = jnp.where(i < n_tiles - 1, next_ref[0], 0.0)

    t = i * tm + lax.broadcasted_iota(jnp.int32, (tm, POOL_GC), 0)
    pooled = []
    for gi, w in enumerate(POOL_WINDOWS):
        cols = slice(gi * POOL_GC, (gi + 1) * POOL_GC)
        acc = ext_ref[halo - w // 2:halo - w // 2 + tm, cols]
        for dlt in range(1, w):
            off = halo - w // 2 + dlt
            acc = acc + ext_ref[off:off + tm, cols]
        cnt = (jnp.clip(t + w // 2, 0, seq_len) - jnp.clip(t - w // 2, 0, seq_len)).astype(F32)
        dg = (acc / cnt - ext_ref[halo:halo + tm, cols]).astype(BF16)
        pooled.append(_dot(dg, pw_ref[gi]))
    pooled = (jnp.concatenate(pooled, axis=1) * ps_ref[...]).astype(BF16)

    d = D_MODEL
    merged = (gates[:, :d] * _dot(attn_ref[0], wbm_ref[...])
              + gates[:, d:] * _dot(pooled, wbp_ref[...]))
    x1 = x + mod[2:3] * _dot(merged.astype(BF16), wo_ref[...])

    h2 = _modulated(x1, g2_ref[...], mod[3:4], mod[4:5]).astype(BF16)
    acc = jnp.zeros(x.shape, F32)
    for c in range(D_FF // FF_CHUNK):
        cols = slice(c * FF_CHUNK, (c + 1) * FF_CHUNK)
        hid = jnp.maximum(_dot(h2, w1_ref[:, cols]), 0.0)
        acc = acc + _dot((hid * hid).astype(BF16), w2_ref[cols, :])
    x2 = x1 + mod[5:6] * acc
    o_ref[0] = _rms(x2, gf_ref[...])


FF_CHUNK = 1024


def _resident_spec(shape):
    nd = len(shape)
    return pl.BlockSpec(shape, lambda *_: (0,) * nd, pipeline_mode=pl.Buffered(1))


def _merge_mlp(x, mod, g1, attn, pool_in, wg, wbm, pw, ps, wbp, wo, g2, w1, w2, gf, tm):
    b, l, d = x.shape
    nt = l // tm
    hb = tm // POOL_HALO
    last_hb = l // POOL_HALO - 1
    kern = functools.partial(_merge_mlp_kernel, seq_len=l)
    weights = (wg, wbm, pw, ps, wbp, wo, g2, w1, w2, gf)
    return pl.pallas_call(
        kern,
        out_shape=jax.ShapeDtypeStruct((b, l, d), F32),
        grid=(b, nt),
        in_specs=[pl.BlockSpec((1, tm, d), lambda bi, i: (bi, i, 0)),
                  pl.BlockSpec((1, 6, d), lambda bi, i: (bi, 0, 0)),
                  _resident_spec(g1.shape),
                  pl.BlockSpec((1, tm, attn.shape[2]), lambda bi, i: (bi, i, 0)),
                  pl.BlockSpec((1, tm, POOL_WIDTH), lambda bi, i: (bi, i, 0)),
                  pl.BlockSpec((1, POOL_HALO, POOL_WIDTH),
                               lambda bi, i: (bi, jnp.maximum(i * hb - 1, 0), 0)),
                  pl.BlockSpec((1, POOL_HALO, POOL_WIDTH),
                               lambda bi, i: (bi, jnp.minimum((i + 1) * hb, last_hb), 0))]
                 + [_resident_spec(w.shape) for w in weights],
        out_specs=pl.BlockSpec((1, tm, d), lambda bi, i: (bi, i, 0)),
        scratch_shapes=[pltpu.VMEM((tm + 2 * POOL_HALO, POOL_WIDTH), F32)],
        compiler_params=pltpu.CompilerParams(
            dimension_semantics=("arbitrary", "arbitrary"), vmem_limit_bytes=VMEM_LIMIT_BYTES),
        name="merge_mlp",
    )(x, mod, g1, attn, pool_in, pool_in, pool_in, *weights)


def _rope_swap_perm():
    half = ROPE_AXIS // 2
    idx = []
    for base in (0, ROPE_AXIS):
        idx += list(range(base + half, base + ROPE_AXIS)) + list(range(base, base + half))
    return np.asarray(idx)


def _rope_tables(seq_len):
    t = np.arange(seq_len)
    row = (t // GRID_W).astype(np.float64)
    col = (t % GRID_W).astype(np.float64)
    inv_freq = ROPE_THETA ** (-np.arange(0, ROPE_AXIS, 2, dtype=np.float64) / ROPE_AXIS)
    ang_row = row[:, None] * inv_freq
    ang_col = col[:, None] * inv_freq
    ones = np.ones((seq_len, QK_NOPE), np.float64)
    pad = np.zeros((seq_len, HEAD_SLAB - QK_NOPE - QK_ROPE), np.float64)
    cos_t = np.concatenate([ones, np.cos(ang_row), np.cos(ang_row),
                            np.cos(ang_col), np.cos(ang_col), pad + 1.0], axis=1)
    sin_t = np.concatenate([0.0 * ones, -np.sin(ang_row), np.sin(ang_row),
                            -np.sin(ang_col), np.sin(ang_col), pad], axis=1)
    return jnp.asarray(cos_t, F32), jnp.asarray(sin_t, F32)


def _pack_weights(w_in, w_uq, w_ukv):
    perm = _rope_swap_perm()
    d = w_in.shape[0]
    s0, s1, s2, s3, s4 = (Q_LORA, Q_LORA + KV_LORA, Q_LORA + KV_LORA + QK_ROPE,
                          Q_LORA + KV_LORA + QK_ROPE + POOL_WIDTH,
                          Q_LORA + KV_LORA + QK_ROPE + POOL_WIDTH + D_MODEL)
    w_kr = w_in[:, s1:s2]
    zl = jnp.zeros((d, QK_NOPE), w_in.dtype)
    zr = jnp.zeros((d, HEAD_SLAB - QK_NOPE - QK_ROPE), w_in.dtype)
    wa = jnp.concatenate([w_in[:, :s1], zl, w_kr, zr, zl, w_kr[:, perm], zr, w_in[:, s2:s3]], axis=1)
    wg = w_in[:, s3:]

    dq = QK_NOPE + QK_ROPE
    wq3 = w_uq.reshape(Q_LORA, N_HEADS, dq)
    zq = jnp.zeros((Q_LORA, N_HEADS, HEAD_SLAB - dq), w_uq.dtype)
    wq_plain = jnp.concatenate([wq3, zq], axis=2).reshape(Q_LORA, N_HEADS * HEAD_SLAB)
    wq_swap = jnp.concatenate([jnp.zeros((Q_LORA, N_HEADS, QK_NOPE), w_uq.dtype),
                               wq3[:, :, QK_NOPE:][:, :, perm], zq], axis=2
                              ).reshape(Q_LORA, N_HEADS * HEAD_SLAB)
    wq = jnp.concatenate([wq_plain, wq_swap], axis=1)

    wkv3 = w_ukv.reshape(KV_LORA, N_HEADS, QK_NOPE + V_DIM)
    wk = jnp.concatenate([wkv3[:, :, :QK_NOPE],
                          jnp.zeros((KV_LORA, N_HEADS, HEAD_SLAB - QK_NOPE), w_ukv.dtype)],
                         axis=2).reshape(KV_LORA, N_HEADS * HEAD_SLAB)
    wvt = wkv3[:, :, QK_NOPE:].reshape(KV_LORA, N_HEADS * V_DIM).T
    return wa.astype(BF16), wg.astype(BF16), wq.astype(BF16), wk.astype(BF16), wvt.astype(BF16)


TOKEN_TILE = 512
Q_TILE = 512


def kernel(x, c, ctx, c_ctx, w_ada, b_ada, norm1_g, w_in, q_norm_g, kv_norm_g, w_uq, w_ukv,
           w_br_mla, pool_w, pool_scale, w_br_pool, w_out, norm2_g, w_mlp1, w_mlp2, final_g):
    b, l, d = x.shape
    assert w_ada.shape[0] == 1, "single-layer block"
    assert l % TOKEN_TILE == 0 and l % Q_TILE == 0 and l % GRID_W == 0

    cond_rows = 16
    cond = jnp.concatenate([c, c_ctx[None, :], jnp.zeros((cond_rows - b - 1, d), F32)], axis=0)
    mod = _adaln(cond, w_ada[0], b_ada[0][None, :]).reshape(cond_rows, 6, d)

    wa, wg, wq, wk, wvt = _pack_weights(w_in[0], w_uq[0], w_ukv[0])
    cos_t, sin_t = _rope_tables(l)
    g1 = norm1_g[0][None, :]
    qg = q_norm_g[0][None, :]
    kvg = kv_norm_g[0][None, :]

    kc, vct = _inproj_ctx(ctx, mod, b, g1, wa, kvg, wk, wvt)
    q, k, vt, pool_in = _inproj_latent(x, mod, g1, wa, qg, kvg, wq, wk, wvt, cos_t, sin_t,
                                       TOKEN_TILE)
    attn = _attention(q, k, vt, kc, vct, Q_TILE)
    return _merge_mlp(x, mod, g1, attn, pool_in, wg, w_br_mla[0].astype(BF16),
                      pool_w[0].astype(BF16), pool_scale[0][None, :], w_br_pool[0].astype(BF16),
                      w_out[0].astype(BF16), norm2_g[0][None, :], w_mlp1[0].astype(BF16),
                      w_mlp2[0].astype(BF16), final_g[None, :], TOKEN_TILE)
```
